```python
import jax, jax.numpy as jnp
from jax import lax
import numpy as np

D_MODEL = 2048
BATCH = 2
SEQ = 16384
DEPTH = 1
DEC_BATCH = 4
DEC_SEQ = 2048
PAST_LEN = 128

N_HEADS = 32
N_KV_HEADS = 8
HEAD_DIM = 64
ATTN_WIDTH = N_HEADS * HEAD_DIM
KV_WIDTH = N_KV_HEADS * HEAD_DIM
WINDOW = 128
BLOCK = 128
ROPE_THETA = 500000.0
ROPE_DIM = HEAD_DIM // 4
SSD_HEADS = 32
SSD_HEAD_DIM = 64
D_INNER = SSD_HEADS * SSD_HEAD_DIM
D_STATE = 128
N_GROUPS = 8
CONV_W = 5
CONV_DIM = D_INNER + 2 * N_GROUPS * D_STATE
CHUNK = 128
D_FF = -(-8 * D_MODEL // (3 * 256)) * 256
N_MIX_BRANCHES = 2
IN_WIDTH = ATTN_WIDTH + 2 * KV_WIDTH + D_INNER + CONV_DIM + 2 * SSD_HEADS + N_MIX_BRANCHES * D_MODEL
EPS = 1e-6

kernel_name = "gated_parallel_swa_ssd_encoder"


def _split_cols(t, widths):
    idx = np.cumsum(widths)[:-1].tolist()
    return jnp.split(t, idx, axis=-1)


def rms_norm(x, w):
    xf = x.astype(jnp.float32)
    y = xf * lax.rsqrt(jnp.mean(xf * xf, axis=-1, keepdims=True) + EPS)
    return (y * w.astype(jnp.float32)).astype(x.dtype)


def partial_rope(t, pos):
    half = ROPE_DIM // 2
    inv = ROPE_THETA ** (-jnp.arange(half, dtype=jnp.float32) * 2.0 / ROPE_DIM)
    ang = pos.astype(jnp.float32)[:, None] * inv[None, :]
    cos = jnp.cos(ang)[None, :, None, :]
    sin = jnp.sin(ang)[None, :, None, :]
    tr = t[..., :ROPE_DIM].astype(jnp.float32)
    x1, x2 = tr[..., :half], tr[..., half:]
    rot = jnp.concatenate([x1 * cos - x2 * sin, x2 * cos + x1 * sin], axis=-1)
    return jnp.concatenate([rot.astype(t.dtype), t[..., ROPE_DIM:]], axis=-1)


def window_attention(q, k, v, sinks):
    b, L = q.shape[0], q.shape[1]
    nblk = L // BLOCK
    grp = N_HEADS // N_KV_HEADS
    qg = q.reshape(b, L, N_KV_HEADS, grp, HEAD_DIM)
    pad = ((0, 0), (BLOCK, BLOCK), (0, 0), (0, 0))
    kp = jnp.pad(k, pad)
    vp = jnp.pad(v, pad)
    scale = HEAD_DIM ** -0.5
    sink = sinks.astype(jnp.float32).reshape(N_KV_HEADS, grp)[None, :, :, None, None]

    def one_block(i):
        start = i * BLOCK
        qb = lax.dynamic_slice_in_dim(qg, start, BLOCK, axis=1)
        kb = lax.dynamic_slice_in_dim(kp, start, 3 * BLOCK, axis=1)
        vb = lax.dynamic_slice_in_dim(vp, start, 3 * BLOCK, axis=1)
        s = jnp.einsum('bqkgd,bskd->bkgqs', qb, kb, preferred_element_type=jnp.float32) * scale
        qpos = start + jnp.arange(BLOCK)
        kpos = start - BLOCK + jnp.arange(3 * BLOCK)
        mask = (jnp.abs(qpos[:, None] - kpos[None, :]) <= WINDOW) & (kpos >= 0)[None, :] & (kpos < L)[None, :]
        s = jnp.where(mask, s, -jnp.inf)
        logits = jnp.concatenate([s, jnp.broadcast_to(sink, s.shape[:-1] + (1,))], axis=-1)
        p = jax.nn.softmax(logits, axis=-1)[..., :-1]
        return jnp.einsum('bkgqs,bskd->bqkgd', p.astype(vb.dtype), vb)

    out = lax.map(one_block, jnp.arange(nblk))
    return jnp.moveaxis(out, 0, 1).reshape(b, L, ATTN_WIDTH)


def ssd_scan(x, dt, A, B, C):
    b, L = x.shape[0], x.shape[1]
    nc = L // CHUNK
    R = SSD_HEADS // N_GROUPS
    X = (x * dt[..., None]).reshape(b, nc, CHUNK, N_GROUPS, R, SSD_HEAD_DIM)
    dA = (dt * A).reshape(b, nc, CHUNK, N_GROUPS, R)
    At = jnp.moveaxis(jnp.cumsum(dA, axis=2), 2, -1)
    Bc = B.reshape(b, nc, CHUNK, N_GROUPS, D_STATE)
    Cc = C.reshape(b, nc, CHUNK, N_GROUPS, D_STATE)
    tril = jnp.tril(jnp.ones((CHUNK, CHUNK), dtype=bool))
    seg = At[..., :, None] - At[..., None, :]
    Lmat = jnp.exp(jnp.where(tril, seg, -jnp.inf))
    CB = jnp.einsum('bclgn,bcsgn->bcgls', Cc, Bc)
    y_diag = jnp.einsum('bcgls,bcgrls,bcsgrp->bclgrp', CB, Lmat, X)
    decay = jnp.exp(At[..., -1:] - At)
    states = jnp.einsum('bclgn,bcgrl,bclgrp->bcgrpn', Bc, decay, X)
    chunk_decay = jnp.exp(At[..., -1])

    def step(s, inp):
        st, dc = inp
        return s * dc[..., None, None] + st, s

    init = jnp.zeros((b, N_GROUPS, R, SSD_HEAD_DIM, D_STATE), jnp.float32)
    _, prev = lax.scan(step, init, (jnp.moveaxis(states, 1, 0), jnp.moveaxis(chunk_decay, 1, 0)))
    prev = jnp.moveaxis(prev, 0, 1)
    y_off = jnp.einsum('bclgn,bcgrpn,bcgrl->bclgrp', Cc, prev, jnp.exp(At))
    return (y_diag + y_off).reshape(b, L, SSD_HEADS, SSD_HEAD_DIM)


def ssd_branch(z, xbc, dt_raw, conv_w, conv_b, a_log_f, a_log_b, dt_bias_f, dt_bias_b, d_skip, ssd_norm_w):
    b, L = z.shape[0], z.shape[1]
    xbc = lax.conv_general_dilated(xbc, conv_w[:, None, :], window_strides=(1,),
                                   padding=[(CONV_W // 2, CONV_W // 2)],
                                   dimension_numbers=('NWC', 'WIO', 'NWC'),
                                   feature_group_count=CONV_DIM) + conv_b
    xbc = jax.nn.silu(xbc.astype(jnp.float32))
    xs, Bm, Cm = _split_cols(xbc, [D_INNER, N_GROUPS * D_STATE, N_GROUPS * D_STATE])
    xh = xs.reshape(b, L, SSD_HEADS, SSD_HEAD_DIM)
    Bm = Bm.reshape(b, L, N_GROUPS, D_STATE)
    Cm = Cm.reshape(b, L, N_GROUPS, D_STATE)
    dtr = dt_raw.astype(jnp.float32)
    dt_f = jax.nn.softplus(dtr[..., :SSD_HEADS] + dt_bias_f.astype(jnp.float32))
    dt_b = jax.nn.softplus(dtr[..., SSD_HEADS:] + dt_bias_b.astype(jnp.float32))
    A_f = -jnp.exp(a_log_f.astype(jnp.float32))
    A_b = -jnp.exp(a_log_b.astype(jnp.float32))
    fl = lambda t: jnp.flip(t, axis=1)
    y_f = ssd_scan(xh, dt_f, A_f, Bm, Cm)
    y_b = fl(ssd_scan(fl(xh), fl(dt_b), A_b, fl(Bm), fl(Cm)))
    y = y_f + y_b + d_skip.astype(jnp.float32)[:, None] * xh
    y = y.reshape(b, L, D_INNER) * jax.nn.silu(z.astype(jnp.float32))
    yg = y.reshape(b, L, N_GROUPS, D_INNER // N_GROUPS)
    yg = yg * lax.rsqrt(jnp.mean(yg * yg, axis=-1, keepdims=True) + EPS)
    y = yg.reshape(b, L, D_INNER) * ssd_norm_w.astype(jnp.float32)
    return y.astype(z.dtype)


def encoder_layer(x, c, w_ada, b_ada, g_pre1, g_post1, w_in, conv_w, conv_b, a_log_f, a_log_b,
                  dt_bias_f, dt_bias_b, d_skip, ssd_norm_w, sinks, w_out, g_pre2, g_post2, w_gu, w_down):
    b, L = x.shape[0], x.shape[1]
    mod = jax.nn.silu(c) @ w_ada + b_ada
    sh1, sc1, gt1, sh2, sc2, gt2 = jnp.split(mod[:, None, :], 6, axis=-1)
    h = rms_norm(x, g_pre1) * (1 + sc1) + sh1
    proj = h @ w_in
    q, k, v, z, xbc, dt_raw, gates = _split_cols(
        proj, [ATTN_WIDTH, KV_WIDTH, KV_WIDTH, D_INNER, CONV_DIM, 2 * SSD_HEADS, N_MIX_BRANCHES * D_MODEL])
    pos = jnp.arange(L)
    q = partial_rope(q.reshape(b, L, N_HEADS, HEAD_DIM), pos)
    k = partial_rope(k.reshape(b, L, N_KV_HEADS, HEAD_DIM), pos)
    v = v.reshape(b, L, N_KV_HEADS, HEAD_DIM)
    attn = window_attention(q, k, v, sinks)
    ssd = ssd_branch(z, xbc, dt_raw, conv_w, conv_b, a_log_f, a_log_b,
                     dt_bias_f, dt_bias_b, d_skip, ssd_norm_w)
    g_a, g_s = jnp.split(jax.nn.sigmoid(gates.astype(jnp.float32)), 2, axis=-1)
    merged = (g_a * attn.astype(jnp.float32) + g_s * ssd.astype(jnp.float32)).astype(x.dtype)
    mix = merged @ w_out
    x = x + gt1 * rms_norm(mix, g_post1)
    h2 = rms_norm(x, g_pre2) * (1 + sc2) + sh2
    gate, up = jnp.split(h2 @ w_gu, 2, axis=-1)
    f = (jax.nn.silu(gate) * up) @ w_down
    return x + gt2 * rms_norm(f, g_post2)


def setup_inputs(seed: int = 0) -> dict:
    key = jax.random.key(seed)
    ks = jax.random.split(key, 24)
    f32 = jnp.float32
    nrm = lambda k, shape, s: jax.random.normal(k, shape, f32) * s
    gain = lambda k, shape: 1.0 + 0.05 * jax.random.normal(k, shape, f32)
    dt0 = lambda k: jnp.exp(jax.random.uniform(k, (DEPTH, SSD_HEADS), f32, np.log(1e-3), np.log(1e-1)))
    inv_softplus = lambda d: d + jnp.log(-jnp.expm1(-d))
    return {
        "x_prompt": nrm(ks[0], (BATCH, SEQ, D_MODEL), 1.0),
        "x_sample": nrm(ks[1], (DEC_BATCH, DEC_SEQ, D_MODEL), 1.0),
        "c_prompt": nrm(ks[2], (BATCH, D_MODEL), 1.0),
        "c_sample": nrm(ks[3], (DEC_BATCH, D_MODEL), 1.0),
        "w_ada": nrm(ks[4], (DEPTH, D_MODEL, 6 * D_MODEL), D_MODEL ** -0.5),
        "b_ada": nrm(ks[5], (DEPTH, 6 * D_MODEL), 0.01),
        "g_pre1": gain(ks[6], (DEPTH, D_MODEL)),
        "g_post1": gain(ks[7], (DEPTH, D_MODEL)),
        "w_in": nrm(ks[8], (DEPTH, D_MODEL, IN_WIDTH), D_MODEL ** -0.5),
        "conv_w": nrm(ks[9], (DEPTH, CONV_W, CONV_DIM), CONV_W ** -0.5),
        "conv_b": nrm(ks[10], (DEPTH, CONV_DIM), 0.01),
        "a_log_f": jnp.log(jax.random.uniform(ks[11], (DEPTH, SSD_HEADS), f32, 1.0, 16.0)),
        "a_log_b": jnp.log(jax.random.uniform(ks[12], (DEPTH, SSD_HEADS), f32, 1.0, 16.0)),
        "dt_bias_f": inv_softplus(dt0(ks[13])),
        "dt_bias_b": inv_softplus(dt0(ks[14])),
        "d_skip": gain(ks[15], (DEPTH, SSD_HEADS)),
        "ssd_norm_w": gain(ks[16], (DEPTH, D_INNER)),
        "sinks": nrm(ks[17], (DEPTH, N_HEADS), 1.0),
        "w_out": nrm(ks[18], (DEPTH, D_MODEL, D_MODEL), D_MODEL ** -0.5),
        "g_pre2": gain(ks[19], (DEPTH, D_MODEL)),
        "g_post2": gain(ks[20], (DEPTH, D_MODEL)),
        "w_gu": nrm(ks[21], (DEPTH, D_MODEL, 2 * D_FF), D_MODEL ** -0.5),
        "w_down": nrm(ks[22], (DEPTH, D_FF, D_MODEL), D_FF ** -0.5),
    }


def reference(x_prompt, x_sample, c_prompt, c_sample, w_ada, b_ada, g_pre1, g_post1, w_in, conv_w, conv_b,
              a_log_f, a_log_b, dt_bias_f, dt_bias_b, d_skip, ssd_norm_w, sinks, w_out, g_pre2, g_post2,
              w_gu, w_down):
    y_prompt = x_prompt
    y_sample = x_sample
    for l in range(DEPTH):
        p = (w_ada[l], b_ada[l], g_pre1[l], g_post1[l], w_in[l], conv_w[l], conv_b[l], a_log_f[l], a_log_b[l],
             dt_bias_f[l], dt_bias_b[l], d_skip[l], ssd_norm_w[l], sinks[l], w_out[l], g_pre2[l], g_post2[l],
             w_gu[l], w_down[l])
        y_prompt = encoder_layer(y_prompt, c_prompt, *p)
        y_sample = encoder_layer(y_sample, c_sample, *p)
    return (y_prompt, y_sample)
```

```python
import functools
import math

import jax
import jax.numpy as jnp
from jax import lax
from jax.experimental import pallas as pl
from jax.experimental.pallas import tpu as pltpu

F32 = jnp.float32
BF16 = jnp.bfloat16

D_MODEL = 2048
N_HEADS = 32
N_KV_HEADS = 8
HEAD_DIM = 64
KV_WIDTH = N_KV_HEADS * HEAD_DIM
BLOCK = 128
ROPE_THETA = 500000.0
ROPE_DIM = HEAD_DIM // 4
SSD_HEADS = 32
SSD_HEAD_DIM = 64
D_INNER = SSD_HEADS * SSD_HEAD_DIM
D_STATE = 128
N_GROUPS = 8
CONV_W = 5
CONV_DIM = D_INNER + 2 * N_GROUPS * D_STATE
CHUNK = 128
D_FF = 5632
EPS = 1e-6

LANES = 128
HALO = 16

COL_XBC = 0
COL_GATE = COL_XBC + CONV_DIM
COL_Q = COL_GATE + 2 * D_MODEL
COL_Z = COL_Q + D_MODEL
COL_K = COL_Z + D_INNER
COL_V = COL_K + KV_WIDTH
PROJ_W = COL_V + KV_WIDTH
DT_W = LANES

VMEM_LIMIT = 56 * 1024 * 1024


def _cparams(sem):
    return pltpu.CompilerParams(dimension_semantics=sem, vmem_limit_bytes=VMEM_LIMIT)


def _silu(v):
    return v * jax.nn.sigmoid(v)


def _mod_kernel(c_ref, w_ref, b_ref, o_ref):
    c = c_ref[...]
    s = _silu(c).astype(BF16)
    o_ref[...] = jnp.dot(s, w_ref[...].astype(BF16), preferred_element_type=F32) + b_ref[...]


def _modulation(c_all, w_ada, b_ada):
    nb = c_all.shape[0]
    n = w_ada.shape[1]
    tn = 1024
    return pl.pallas_call(
        _mod_kernel,
        grid=(n // tn,),
        in_specs=[pl.BlockSpec((nb, D_MODEL), lambda j: (0, 0)),
                  pl.BlockSpec((D_MODEL, tn), lambda j: (0, j)),
                  pl.BlockSpec((1, tn), lambda j: (0, j))],
        out_specs=pl.BlockSpec((nb, tn), lambda j: (0, j)),
        out_shape=jax.ShapeDtypeStruct((nb, n), F32),
        compiler_params=_cparams(("arbitrary",)),
        name="adaln_mod",
    )(c_all, w_ada, b_ada.reshape(1, n))


def _rope_table_kernel(cos_ref, sin_lo_ref, sin_hi_ref):
    tm = cos_ref.shape[0]
    half = ROPE_DIM // 2
    pos = (pl.program_id(0) * tm + lax.broadcasted_iota(jnp.int32, (tm, LANES), 0)).astype(F32)
    d = lax.broadcasted_iota(jnp.int32, (tm, LANES), 1) % HEAD_DIM
    fidx = (d % half).astype(F32)
    inv = jnp.exp(fidx * (-2.0 / ROPE_DIM * math.log(ROPE_THETA)))
    ang = pos * inv
    cos = jnp.cos(ang)
    sin = jnp.sin(ang)
    cos_ref[...] = jnp.where(d < ROPE_DIM, cos, 1.0)
    sin_lo_ref[...] = jnp.where(d < half, -sin, 0.0)
    sin_hi_ref[...] = jnp.where((d >= half) & (d < ROPE_DIM), sin, 0.0)


def _rope_tables(seq):
    tm = min(seq, 2048)
    shp = jax.ShapeDtypeStruct((seq, LANES), F32)
    spec = pl.BlockSpec((tm, LANES), lambda i: (i, 0))
    return pl.pallas_call(
        _rope_table_kernel,
        grid=(seq // tm,),
        in_specs=[],
        out_specs=[spec, spec, spec],
        out_shape=[shp, shp, shp],
        compiler_params=_cparams(("arbitrary",)),
        name="rope_tables",
    )()


def _rope_slab(t, cos, sin_lo, sin_hi):
    half = ROPE_DIM // 2
    return t * cos + pltpu.roll(t, LANES - half, 1) * sin_lo + pltpu.roll(t, half, 1) * sin_hi


def _inproj_kernel(x_ref, sc_ref, sh_ref, g_ref, w_ref, wdt_ref, cos_ref, slo_ref, shi_ref,
                   proj_ref, dt_ref, h_scr, *, tn):
    j = pl.program_id(1)

    @pl.when(j == 0)
    def _():
        x = x_ref[...]
        ms = jnp.mean(x * x, axis=-1, keepdims=True)
        y = x * lax.rsqrt(ms + EPS) * g_ref[...]
        h = (y * (1.0 + sc_ref[...]) + sh_ref[...]).astype(BF16)
        h_scr[...] = h
        dt_ref[...] = jnp.dot(h, wdt_ref[...], preferred_element_type=F32)

    acc = jnp.dot(h_scr[...], w_ref[...], preferred_element_type=F32)
    q_lo, q_hi = COL_Q // tn, (COL_Q + D_MODEL) // tn
    k_tile = COL_K // tn
    k_slabs = KV_WIDTH // LANES
    is_q = (j >= q_lo) & (j < q_hi)
    is_k = j == k_tile

    def rope_store(n_slabs, scale):
        cos, slo, shi = cos_ref[...], slo_ref[...], shi_ref[...]
        for s in range(tn // LANES):
            t = acc[:, s * LANES:(s + 1) * LANES]
            if s < n_slabs:
                t = _rope_slab(t, cos, slo, shi)
                if scale != 1.0:
                    t = t * scale
            proj_ref[:, s * LANES:(s + 1) * LANES] = t.astype(BF16)

    @pl.when(is_q)
    def _():
        rope_store(tn // LANES, HEAD_DIM ** -0.5)

    @pl.when(is_k)
    def _():
        rope_store(k_slabs, 1.0)

    @pl.when(jnp.logical_not(is_q | is_k))
    def _():
        proj_ref[...] = acc.astype(BF16)


def _in_projection(x2, mod4, g_pre, w_main, w_dt, tables, batch, seq):
    t_rows = batch * seq
    tm = min(seq, 1024)
    tn = 1024
    assert COL_K % tn == 0 and COL_Q % tn == 0 and D_MODEL % tn == 0 and PROJ_W % tn == 0
    per_b = seq // tm
    cos, slo, shi = tables
    tab_spec = pl.BlockSpec((tm, LANES), lambda i, j: (i % per_b, 0))
    mod_spec = lambda k: pl.BlockSpec((None, None, 1, D_MODEL), lambda i, j: (i // per_b, k, 0, 0))
    return pl.pallas_call(
        functools.partial(_inproj_kernel, tn=tn),
        grid=(t_rows // tm, PROJ_W // tn),
        in_specs=[pl.BlockSpec((tm, D_MODEL), lambda i, j: (i, 0)),
                  mod_spec(1), mod_spec(0),
                  pl.BlockSpec((1, D_MODEL), lambda i, j: (0, 0)),
                  pl.BlockSpec((D_MODEL, tn), lambda i, j: (0, j)),
                  pl.BlockSpec((D_MODEL, DT_W), lambda i, j: (0, 0)),
                  tab_spec, tab_spec, tab_spec],
        out_specs=[pl.BlockSpec((tm, tn), lambda i, j: (i, j)),
                   pl.BlockSpec((tm, DT_W), lambda i, j: (i, 0))],
        out_shape=[jax.ShapeDtypeStruct((t_rows, PROJ_W), BF16),
                   jax.ShapeDtypeStruct((t_rows, DT_W), F32)],
        scratch_shapes=[pltpu.VMEM((tm, D_MODEL), BF16)],
        compiler_params=_cparams(("arbitrary", "arbitrary")),
        name="norm_in_proj",
    )(x2, mod4, mod4, g_pre.reshape(1, D_MODEL), w_main, w_dt, cos, slo, shi)


def _attn_kernel(sink_ref, q_ref, kp_ref, kc_ref, kn_ref, vp_ref, vc_ref, vn_ref, o_ref):
    i = pl.program_id(1)
    nblk = pl.num_programs(1)
    r = lax.broadcasted_iota(jnp.int32, (BLOCK, BLOCK), 0)
    c = lax.broadcasted_iota(jnp.int32, (BLOCK, BLOCK), 1)
    neg = jnp.float32(-jnp.inf)
    bias = jnp.concatenate([
        jnp.where((c >= r) & (i > 0), 0.0, neg),
        jnp.zeros((BLOCK, BLOCK), F32),
        jnp.where((c <= r) & (i < nblk - 1), 0.0, neg)], axis=1)
    grp = N_HEADS // N_KV_HEADS
    for g in range(N_KV_HEADS):
        sl = slice(g * HEAD_DIM, (g + 1) * HEAD_DIM)
        k3 = jnp.concatenate([kp_ref[:, sl], kc_ref[:, sl], kn_ref[:, sl]], axis=0)
        v3 = jnp.concatenate([vp_ref[:, sl], vc_ref[:, sl], vn_ref[:, sl]], axis=0)
        for rr in range(grp):
            h = g * grp + rr
            hs = slice(h * HEAD_DIM, (h + 1) * HEAD_DIM)
            s = lax.dot_general(q_ref[:, hs], k3, (((1,), (1,)), ((), ())),
                                preferred_element_type=F32) + bias
            sink = sink_ref[h]
            m = jnp.maximum(jnp.max(s, axis=-1, keepdims=True), sink)
            p = jnp.exp(s - m)
            denom = jnp.sum(p, axis=-1, keepdims=True) + jnp.exp(sink - m)
            o = jnp.dot(p.astype(BF16), v3, preferred_element_type=F32) / denom
            o_ref[:, hs] = o.astype(BF16)


def _attention(proj3, sinks):
    batch, seq, _ = proj3.shape
    nblk = seq // BLOCK
    kcol, vcol = COL_K // KV_WIDTH, COL_V // KV_WIDTH
    prev = lambda col: pl.BlockSpec((None, BLOCK, KV_WIDTH), lambda b, i: (b, jnp.maximum(i - 1, 0), col))
    cur = lambda col: pl.BlockSpec((None, BLOCK, KV_WIDTH), lambda b, i: (b, i, col))
    nxt = lambda col: pl.BlockSpec((None, BLOCK, KV_WIDTH), lambda b, i: (b, jnp.minimum(i + 1, nblk - 1), col))
    return pl.pallas_call(
        _attn_kernel,
        grid=(batch, nblk),
        in_specs=[pl.BlockSpec(memory_space=pltpu.SMEM),
                  pl.BlockSpec((None, BLOCK, D_MODEL), lambda b, i: (b, i, COL_Q // D_MODEL)),
                  prev(kcol), cur(kcol), nxt(kcol), prev(vcol), cur(vcol), nxt(vcol)],
        out_specs=pl.BlockSpec((None, BLOCK, D_MODEL), lambda b, i: (b, i, 0)),
        out_shape=jax.ShapeDtypeStruct((batch, seq, D_MODEL), BF16),
        compiler_params=_cparams(("arbitrary", "arbitrary")),
        name="window_attn",
    )(sinks, proj3, proj3, proj3, proj3, proj3, proj3, proj3)


def _ssd_direction(xm_ref, xp_ref, xn_ref, dt_ref, cw_ref, cb_ref, bias_ref, a_ref, e_ref, dskip_ref,
                   st_ref, out_ref, *, chunk_idx, n_chunks, reverse):
    lane0 = SSD_HEADS if reverse else 0

    has_prev = (chunk_idx > 0).astype(F32)
    has_next = (chunk_idx < n_chunks - 1).astype(F32)
    full = jnp.concatenate([xp_ref[...].astype(F32) * has_prev,
                            xm_ref[...].astype(F32),
                            xn_ref[...].astype(F32) * has_next], axis=0)
    rows = CHUNK + 2 * HALO
    acc = jnp.zeros((CHUNK, CONV_DIM), F32) + cb_ref[...]
    for w in range(CONV_W):
        shift = (CONV_W // 2 - w) % rows
        win = full if shift == 0 else pltpu.roll(full, shift, 0)
        acc = acc + win[HALO:HALO + CHUNK] * cw_ref[w:w + 1, :]
    xbc = _silu(acc)
    xs = xbc[:, :D_INNER]
    bm = xbc[:, D_INNER:D_INNER + N_GROUPS * D_STATE].astype(BF16)
    cm = xbc[:, D_INNER + N_GROUPS * D_STATE:].astype(BF16)

    dt = jax.nn.softplus(dt_ref[...] + bias_ref[...])
    da = dt * a_ref[...]
    r = lax.broadcasted_iota(jnp.int32, (CHUNK, CHUNK), 0)
    c = lax.broadcasted_iota(jnp.int32, (CHUNK, CHUNK), 1)
    causal = (c >= r) if reverse else (c <= r)
    cum = jnp.dot(causal.astype(F32), da, preferred_element_type=F32, precision=lax.Precision.HIGHEST)
    cum_t = cum.T
    edge = CHUNK - 1 if not reverse else 0
    total = cum[edge:edge + 1, :]
    exp_cum = jnp.exp(cum)
    decay = jnp.exp(total - cum)

    e = e_ref[...]

    def expand(v):
        return jnp.dot(v.astype(BF16), e, preferred_element_type=F32)

    def expand_hi_lo(v):
        hi = v.astype(BF16)
        lo = (v - hi.astype(F32)).astype(BF16)
        return jnp.dot(hi, e, preferred_element_type=F32) + jnp.dot(lo, e, preferred_element_type=F32)

    x_dt = (xs * expand(dt)).astype(BF16)
    x_dec = (xs * expand(dt * decay)).astype(BF16)
    exp_cum_e = expand_hi_lo(exp_cum)

    heads_per_group = SSD_HEADS // N_GROUPS
    gw = heads_per_group * SSD_HEAD_DIM
    lane = lax.broadcasted_iota(jnp.int32, (CHUNK, LANES), 1)
    neg = jnp.float32(-jnp.inf)
    for g in range(N_GROUPS):
        b_g = bm[:, g * D_STATE:(g + 1) * D_STATE]
        c_g = cm[:, g * D_STATE:(g + 1) * D_STATE]
        cb = lax.dot_general(c_g, b_g, (((1,), (1,)), ((), ())), preferred_element_type=F32)
        gs = slice(g * gw, (g + 1) * gw)
        st_prev = st_ref[:, gs]
        y = jnp.dot(c_g, st_prev.astype(BF16), preferred_element_type=F32) * exp_cum_e[:, gs]
        y_diag = []
        for pair in range(heads_per_group // 2):
            ms = []
            for hh in range(2):
                k = lane0 + g * heads_per_group + pair * 2 + hh
                seg = cum[:, k:k + 1] - cum_t[k:k + 1, :]
                lmat = jnp.exp(jnp.where(causal, seg, neg))
                ms.append((cb * lmat).astype(BF16))
            xp = x_dt[:, g * gw + pair * LANES: g * gw + (pair + 1) * LANES]
            zero = jnp.zeros_like(xp)
            rhs = jnp.concatenate([jnp.where(lane < SSD_HEAD_DIM, xp, zero),
                                   jnp.where(lane >= SSD_HEAD_DIM, xp, zero)], axis=0)
            y_diag.append(jnp.dot(jnp.concatenate(ms, axis=1), rhs, preferred_element_type=F32))
        y = y + jnp.concatenate(y_diag, axis=1)
        if dskip_ref is not None:
            y = y + dskip_ref[:, gs] * xs[:, gs]
        out_ref[:, gs] = y.astype(out_ref.dtype)
        upd = jnp.dot(b_g.astype(F32).T.astype(BF16), x_dec[:, gs], preferred_element_type=F32)
        st_ref[:, gs] = st_prev * exp_cum_e[edge:edge + 1, gs] + upd


def _ssd_kernel(xm_f, xp_f, xn_f, dt_f, xm_b, xp_b, xn_b, dt_b, cw_ref, cb_ref, bias_ref, a_ref, e_f, e_b,
                dskip_ref, yf_ref, yb_ref, st_f, st_b):
    ci = pl.program_id(1)
    nc = pl.num_programs(1)

    @pl.when(ci == 0)
    def _():
        st_f[...] = jnp.zeros_like(st_f)
        st_b[...] = jnp.zeros_like(st_b)

    _ssd_direction(xm_f, xp_f, xn_f, dt_f, cw_ref, cb_ref, bias_ref, a_ref, e_f, dskip_ref, st_f, yf_ref,
                   chunk_idx=ci, n_chunks=nc, reverse=False)
    _ssd_direction(xm_b, xp_b, xn_b, dt_b, cw_ref, cb_ref, bias_ref, a_ref, e_b, None, st_b, yb_ref,
                   chunk_idx=nc - 1 - ci, n_chunks=nc, reverse=True)


def _ssd_scan(proj3, dt3, conv_w8, conv_b, dt_bias, a_neg, e_f, e_b, dskip_e):
    batch, seq, _ = proj3.shape
    nc = seq // CHUNK
    hpc = CHUNK // HALO
    nh = seq // HALO

    def specs(chunk_of):
        main = pl.BlockSpec((None, CHUNK, CONV_DIM), lambda b, c: (b, chunk_of(c), COL_XBC // CONV_DIM))
        prev = pl.BlockSpec((None, HALO, CONV_DIM),
                            lambda b, c: (b, jnp.maximum(chunk_of(c) * hpc - 1, 0), COL_XBC // CONV_DIM))
        nxt = pl.BlockSpec((None, HALO, CONV_DIM),
                           lambda b, c: (b, jnp.minimum((chunk_of(c) + 1) * hpc, nh - 1), COL_XBC // CONV_DIM))
        dts = pl.BlockSpec((None, CHUNK, DT_W), lambda b, c: (b, chunk_of(c), 0))
        return [main, prev, nxt, dts]

    fwd = lambda c: c
    bwd = lambda c: nc - 1 - c
    const = lambda shape: pl.BlockSpec(shape, lambda b, c: (0,) * len(shape))
    out_f = pl.BlockSpec((None, CHUNK, D_INNER), lambda b, c: (b, c, 0))
    out_b = pl.BlockSpec((None, CHUNK, D_INNER), lambda b, c: (b, nc - 1 - c, 0))
    y_shape = jax.ShapeDtypeStruct((batch, seq, D_INNER), BF16)
    return pl.pallas_call(
        _ssd_kernel,
        grid=(batch, nc),
        in_specs=specs(fwd) + specs(bwd) + [
            const((8, CONV_DIM)), const((1, CONV_DIM)), const((1, DT_W)), const((1, DT_W)),
            const((DT_W, D_INNER)), const((DT_W, D_INNER)), const((1, D_INNER))],
        out_specs=[out_f, out_b],
        out_shape=[y_shape, y_shape],
        scratch_shapes=[pltpu.VMEM((D_STATE, D_INNER), F32), pltpu.VMEM((D_STATE, D_INNER), F32)],
        compiler_params=_cparams(("arbitrary", "arbitrary")),
        name="ssd_scan",
    )(proj3, proj3, proj3, dt3, proj3, proj3, proj3, dt3, conv_w8, conv_b, dt_bias, a_neg, e_f, e_b, dskip_e)


def _outproj_kernel(x_ref, gt_ref, yf_ref, yb_ref, z_ref, attn_ref, ga_ref, gs_ref, nw_ref, w_ref, gpost_ref,
                    o_ref):
    y = (yf_ref[...].astype(F32) + yb_ref[...].astype(F32)) * _silu(z_ref[...].astype(F32))
    gw = D_INNER // N_GROUPS
    parts = []
    for g in range(N_GROUPS):
        yg = y[:, g * gw:(g + 1) * gw]
        parts.append(yg * lax.rsqrt(jnp.mean(yg * yg, axis=-1, keepdims=True) + EPS))
    ssd = jnp.concatenate(parts, axis=1) * nw_ref[...]
    merged = (jax.nn.sigmoid(ga_ref[...].astype(F32)) * attn_ref[...].astype(F32)
              + jax.nn.sigmoid(gs_ref[...].astype(F32)) * ssd)
    mix = jnp.dot(merged.astype(BF16), w_ref[...], preferred_element_type=F32)
    nrm = mix * lax.rsqrt(jnp.mean(mix * mix, axis=-1, keepdims=True) + EPS) * gpost_ref[...]
    o_ref[...] = x_ref[...] + gt_ref[...] * nrm


def _out_projection(x2, mod4, yf2, yb2, proj2, attn2, ssd_norm_w, w_out, g_post, batch, seq):
    t_rows = batch * seq
    tm = min(seq, 256)
    per_b = seq // tm
    row = lambda col: pl.BlockSpec((tm, D_MODEL), lambda i: (i, col))
    vec = pl.BlockSpec((1, D_MODEL), lambda i: (0, 0))
    return pl.pallas_call(
        _outproj_kernel,
        grid=(t_rows // tm,),
        in_specs=[row(0),
                  pl.BlockSpec((None, None, 1, D_MODEL), lambda i: (i // per_b, 2, 0, 0)),
                  row(0), row(0), row(COL_Z // D_MODEL), row(0),
                  row(COL_GATE // D_MODEL), row(COL_GATE // D_MODEL + 1),
                  vec,
                  pl.BlockSpec((D_MODEL, D_MODEL), lambda i: (0, 0)),
                  vec],
        out_specs=row(0),
        out_shape=jax.ShapeDtypeStruct((t_rows, D_MODEL), F32),
        compiler_params=_cparams(("arbitrary",)),
        name="merge_out_proj",
    )(x2, mod4, yf2, yb2, proj2, attn2, proj2, proj2, ssd_norm_w.reshape(1, D_INNER), w_out,
      g_post.reshape(1, D_MODEL))


def _ffn_kernel(x_ref, sc_ref, sh_ref, gt_ref, gpre_ref, wg_ref, wu_ref, wd_ref, gpost_ref, o_ref, h_scr, acc_scr):
    j = pl.program_id(1)

    @pl.when(j == 0)
    def _():
        x = x_ref[...]
        ms = jnp.mean(x * x, axis=-1, keepdims=True)
        y = x * lax.rsqrt(ms + EPS) * gpre_ref[...]
        h_scr[...] = (y * (1.0 + sc_ref[...]) + sh_ref[...]).astype(BF16)
        acc_scr[...] = jnp.zeros_like(acc_scr)

    h = h_scr[...]
    gate = jnp.dot(h, wg_ref[...], preferred_element_type=F32)
    up = jnp.dot(h, wu_ref[...], preferred_element_type=F32)
    act = (_silu(gate) * up).astype(BF16)
    acc_scr[...] += jnp.dot(act, wd_ref[...], preferred_element_type=F32)

    @pl.when(j == pl.num_programs(1) - 1)
    def _():
        f = acc_scr[...]
        nrm = f * lax.rsqrt(jnp.mean(f * f, axis=-1, keepdims=True) + EPS) * gpost_ref[...]
        o_ref[...] = x_ref[...] + gt_ref[...] * nrm


def _ffn(x2, mod4, g_pre, w_g, w_u, w_d, g_post, batch, seq):
    t_rows = batch * seq
    tm = min(seq, 512)
    tf = 512
    per_b = seq // tm
    mod_spec = lambda k: pl.BlockSpec((None, None, 1, D_MODEL), lambda i, j: (i // per_b, k, 0, 0))
    vec = pl.BlockSpec((1, D_MODEL), lambda i, j: (0, 0))
    return pl.pallas_call(
        _ffn_kernel,
        grid=(t_rows // tm, D_FF // tf),
        in_specs=[pl.BlockSpec((tm, D_MODEL), lambda i, j: (i, 0)),
                  mod_spec(4), mod_spec(3), mod_spec(5), vec,
                  pl.BlockSpec((D_MODEL, tf), lambda i, j: (0, j)),
                  pl.BlockSpec((D_MODEL, tf), lambda i, j: (0, j)),
                  pl.BlockSpec((tf, D_MODEL), lambda i, j: (j, 0)),
                  vec],
        out_specs=pl.BlockSpec((tm, D_MODEL), lambda i, j: (i, 0)),
        out_shape=jax.ShapeDtypeStruct((t_rows, D_MODEL), F32),
        scratch_shapes=[pltpu.VMEM((tm, D_MODEL), BF16), pltpu.VMEM((tm, D_MODEL), F32)],
        compiler_params=_cparams(("arbitrary", "arbitrary")),
        name="swiglu_ffn",
    )(x2, mod4, mod4, mod4, g_pre.reshape(1, D_MODEL), w_g, w_u, w_d, g_post.reshape(1, D_MODEL))


def _prep_weights(w_in, conv_w, conv_b, a_log_f, a_log_b, dt_bias_f, dt_bias_b, d_skip, w_out, w_gu, w_down):
    o_q, o_k, o_v = 0, D_MODEL, D_MODEL + KV_WIDTH
    o_z = o_v + KV_WIDTH
    o_xbc = o_z + D_INNER
    o_dt = o_xbc + CONV_DIM
    o_gate = o_dt + 2 * SSD_HEADS
    seg = lambda off, width: w_in[:, off:off + width]
    w_main = jnp.concatenate([seg(o_xbc, CONV_DIM), seg(o_gate, 2 * D_MODEL), seg(o_q, D_MODEL),
                              seg(o_z, D_INNER), seg(o_k, KV_WIDTH), seg(o_v, KV_WIDTH)], axis=1).astype(BF16)
    w_dt = jnp.pad(seg(o_dt, 2 * SSD_HEADS), ((0, 0), (0, DT_W - 2 * SSD_HEADS))).astype(BF16)
    pad_l = lambda v: jnp.pad(v, (0, DT_W - v.shape[0])).reshape(1, DT_W)
    dt_bias = pad_l(jnp.concatenate([dt_bias_f, dt_bias_b]))
    a_neg = pad_l(-jnp.exp(jnp.concatenate([a_log_f, a_log_b])))
    conv_w8 = jnp.pad(conv_w, ((0, 8 - CONV_W), (0, 0)))
    head_of_lane = jnp.arange(D_INNER) // SSD_HEAD_DIM
    rows = jnp.arange(DT_W)[:, None]
    e_f = (rows == head_of_lane[None, :]).astype(BF16)
    e_b = (rows == head_of_lane[None, :] + SSD_HEADS).astype(BF16)
    dskip_e = jnp.repeat(d_skip, SSD_HEAD_DIM).reshape(1, D_INNER)
    return dict(w_main=w_main, w_dt=w_dt, dt_bias=dt_bias, a_neg=a_neg, conv_w8=conv_w8,
                conv_b=conv_b.reshape(1, CONV_DIM), e_f=e_f, e_b=e_b, dskip_e=dskip_e,
                w_out=w_out.astype(BF16), w_g=w_gu[:, :D_FF].astype(BF16), w_u=w_gu[:, D_FF:].astype(BF16),
                w_d=w_down.astype(BF16))


def _encoder_layer(x, mod4, tables, pw, g_pre1, g_post1, ssd_norm_w, sinks, g_pre2, g_post2):
    batch, seq, _ = x.shape
    x2 = x.reshape(batch * seq, D_MODEL)
    proj2, dt2 = _in_projection(x2, mod4, g_pre1, pw["w_main"], pw["w_dt"], tables, batch, seq)
    proj3 = proj2.reshape(batch, seq, PROJ_W)
    dt3 = dt2.reshape(batch, seq, DT_W)
    attn = _attention(proj3, sinks)
    yf, yb = _ssd_scan(proj3, dt3, pw["conv_w8"], pw["conv_b"], pw["dt_bias"], pw["a_neg"], pw["e_f"], pw["e_b"],
                       pw["dskip_e"])
    x1 = _out_projection(x2, mod4, yf.reshape(batch * seq, D_INNER), yb.reshape(batch * seq, D_INNER), proj2,
                         attn.reshape(batch * seq, D_MODEL), ssd_norm_w, pw["w_out"], g_post1, batch, seq)
    out = _ffn(x1, mod4, g_pre2, pw["w_g"], pw["w_u"], pw["w_d"], g_post2, batch, seq)
    return out.reshape(batch, seq, D_MODEL)


def kernel(x_prompt, x_sample, c_prompt, c_sample, w_ada, b_ada, g_pre1, g_post1, w_in, conv_w, conv_b, a_log_f,
           a_log_b, dt_bias_f, dt_bias_b, d_skip, ssd_norm_w, sinks, w_out, g_pre2, g_post2, w_gu, w_down):
    assert w_ada.shape[0] == 1, "single-layer problem"
    nb_p, nb_s = c_prompt.shape[0], c_sample.shape[0]
    nb = -(-(nb_p + nb_s) // 8) * 8
    c_all = jnp.pad(jnp.concatenate([c_prompt, c_sample], axis=0), ((0, nb - nb_p - nb_s), (0, 0)))
    mod = _modulation(c_all, w_ada[0], b_ada[0]).reshape(nb, 6, 1, D_MODEL)
    pw = _prep_weights(w_in[0], conv_w[0], conv_b[0], a_log_f[0], a_log_b[0], dt_bias_f[0], dt_bias_b[0],
                       d_skip[0], w_out[0], w_gu[0], w_down[0])
    outs = []
    for x, mod4 in ((x_prompt, mod[:nb_p]), (x_sample, mod[nb_p:nb_p + nb_s])):
        tables = _rope_tables(x.shape[1])
        outs.append(_encoder_layer(x, mod4, tables, pw, g_pre1[0], g_post1[0], ssd_norm_w[0], sinks[0],
                                   g_pre2[0], g_post2[0]))
    return tuple(outs)
```

```python
import functools
import math

import jax
import jax.numpy as jnp
from jax import lax
from jax.experimental import pallas as pl
from jax.experimental.pallas import tpu as pltpu

F32 = jnp.float32
BF16 = jnp.bfloat16

D_MODEL = 2048
N_HEADS = 32
N_KV_HEADS = 8
HEAD_DIM = 64
KV_WIDTH = N_KV_HEADS * HEAD_DIM
BLOCK = 128
ROPE_THETA = 500000.0
ROPE_DIM = HEAD_DIM // 4
SSD_HEADS = 32
SSD_HEAD_DIM = 64
D_INNER = SSD_HEADS * SSD_HEAD_DIM
D_STATE = 128
N_GROUPS = 8
CONV_W = 5
CONV_DIM = D_INNER + 2 * N_GROUPS * D_STATE
CHUNK = 128
D_FF = 5632
EPS = 1e-6

LANES = 128
HALO = 16

COL_XBC = 0
COL_GATE = COL_XBC + CONV_DIM
COL_Q = COL_GATE + 2 * D_MODEL
COL_Z = COL_Q + D_MODEL
COL_K = COL_Z + D_INNER
COL_V = COL_K + KV_WIDTH
PROJ_W = COL_V + KV_WIDTH
DT_W = LANES

VMEM_LIMIT = 56 * 1024 * 1024


def _cparams(sem):
    return pltpu.CompilerParams(dimension_semantics=sem, vmem_limit_bytes=VMEM_LIMIT)


def _silu(v):
    return v * jax.nn.sigmoid(v)


def _mod_kernel(c_ref, w_ref, b_ref, o_ref):
    c = c_ref[...]
    s = _silu(c).astype(BF16)
    o_ref[...] = jnp.dot(s, w_ref[...].astype(BF16), preferred_element_type=F32) + b_ref[...]


def _modulation(c_all, w_ada, b_ada):
    nb = c_all.shape[0]
    n = w_ada.shape[1]
    tn = 1024
    return pl.pallas_call(
        _mod_kernel,
        grid=(n // tn,),
        in_specs=[pl.BlockSpec((nb, D_MODEL), lambda j: (0, 0)),
                  pl.BlockSpec((D_MODEL, tn), lambda j: (0, j)),
                  pl.BlockSpec((1, tn), lambda j: (0, j))],
        out_specs=pl.BlockSpec((nb, tn), lambda j: (0, j)),
        out_shape=jax.ShapeDtypeStruct((nb, n), F32),
        compiler_params=_cparams(("arbitrary",)),
        name="adaln_mod",
    )(c_all, w_ada, b_ada.reshape(1, n))


def _rope_table_kernel(cos_ref, sin_lo_ref, sin_hi_ref):
    tm = cos_ref.shape[0]
    half = ROPE_DIM // 2
    pos = (pl.program_id(0) * tm + lax.broadcasted_iota(jnp.int32, (tm, LANES), 0)).astype(F32)
    d = lax.broadcasted_iota(jnp.int32, (tm, LANES), 1) % HEAD_DIM
    fidx = (d % half).astype(F32)
    inv = jnp.exp(fidx * (-2.0 / ROPE_DIM * math.log(ROPE_THETA)))
    ang = pos * inv
    cos = jnp.cos(ang)
    sin = jnp.sin(ang)
    cos_ref[...] = jnp.where(d < ROPE_DIM, cos, 1.0)
    sin_lo_ref[...] = jnp.where(d < half, -sin, 0.0)
    sin_hi_ref[...] = jnp.where((d >= half) & (d < ROPE_DIM), sin, 0.0)


def _rope_tables(seq):
    tm = min(seq, 2048)
    shp = jax.ShapeDtypeStruct((seq, LANES), F32)
    spec = pl.BlockSpec((tm, LANES), lambda i: (i, 0))
    return pl.pallas_call(
        _rope_table_kernel,
        grid=(seq // tm,),
        in_specs=[],
        out_specs=[spec, spec, spec],
        out_shape=[shp, shp, shp],
        compiler_params=_cparams(("arbitrary",)),
        name="rope_tables",
    )()


def _rope_slab(t, cos, sin_lo, sin_hi):
    half = ROPE_DIM // 2
    return t * cos + pltpu.roll(t, LANES - half, 1) * sin_lo + pltpu.roll(t, half, 1) * sin_hi


def _inproj_kernel(x_ref, sc_ref, sh_ref, g_ref, w_ref, wdt_ref, cos_ref, slo_ref, shi_ref,
                   proj_ref, dt_ref, h_scr, *, tn):
    j = pl.program_id(1)

    @pl.when(j == 0)
    def _():
        x = x_ref[...]
        ms = jnp.mean(x * x, axis=-1, keepdims=True)
        y = x * lax.rsqrt(ms + EPS) * g_ref[...]
        h = (y * (1.0 + sc_ref[...]) + sh_ref[...]).astype(BF16)
        h_scr[...] = h
        dt_ref[...] = jnp.dot(h, wdt_ref[...], preferred_element_type=F32)

    acc = jnp.dot(h_scr[...], w_ref[...], preferred_element_type=F32)
    q_lo, q_hi = COL_Q // tn, (COL_Q + D_MODEL) // tn
    k_tile = COL_K // tn
    k_slabs = KV_WIDTH // LANES
    is_q = (j >= q_lo) & (j < q_hi)
    is_k = j == k_tile

    def rope_store(n_slabs, scale):
        cos, slo, shi = cos_ref[...], slo_ref[...], shi_ref[...]
        for s in range(tn // LANES):
            t = acc[:, s * LANES:(s + 1) * LANES]
            if s < n_slabs:
                t = _rope_slab(t, cos, slo, shi)
                if scale != 1.0:
                    t = t * scale
            proj_ref[:, s * LANES:(s + 1) * LANES] = t.astype(BF16)

    @pl.when(is_q)
    def _():
        rope_store(tn // LANES, HEAD_DIM ** -0.5)

    @pl.when(is_k)
    def _():
        rope_store(k_slabs, 1.0)

    @pl.when(jnp.logical_not(is_q | is_k))
    def _():
        proj_ref[...] = acc.astype(BF16)


def _in_projection(x2, mod4, g_pre, w_main, w_dt, tables, batch, seq):
    t_rows = batch * seq
    tm = min(seq, 1024)
    tn = 1024
    assert COL_K % tn == 0 and COL_Q % tn == 0 and D_MODEL % tn == 0 and PROJ_W % tn == 0
    per_b = seq // tm
    cos, slo, shi = tables
    tab_spec = pl.BlockSpec((tm, LANES), lambda i, j: (i % per_b, 0))
    mod_spec = lambda k: pl.BlockSpec((None, None, 1, D_MODEL), lambda i, j: (i // per_b, k, 0, 0))
    return pl.pallas_call(
        functools.partial(_inproj_kernel, tn=tn),
        grid=(t_rows // tm, PROJ_W // tn),
        in_specs=[pl.BlockSpec((tm, D_MODEL), lambda i, j: (i, 0)),
                  mod_spec(1), mod_spec(0),
                  pl.BlockSpec((1, D_MODEL), lambda i, j: (0, 0)),
                  pl.BlockSpec((D_MODEL, tn), lambda i, j: (0, j)),
                  pl.BlockSpec((D_MODEL, DT_W), lambda i, j: (0, 0)),
                  tab_spec, tab_spec, tab_spec],
        out_specs=[pl.BlockSpec((tm, tn), lambda i, j: (i, j)),
                   pl.BlockSpec((tm, DT_W), lambda i, j: (i, 0))],
        out_shape=[jax.ShapeDtypeStruct((t_rows, PROJ_W), BF16),
                   jax.ShapeDtypeStruct((t_rows, DT_W), F32)],
        scratch_shapes=[pltpu.VMEM((tm, D_MODEL), BF16)],
        compiler_params=_cparams(("arbitrary", "arbitrary")),
        name="norm_in_proj",
    )(x2, mod4, mod4, g_pre.reshape(1, D_MODEL), w_main, w_dt, cos, slo, shi)


def _attn_kernel(sink_ref, q_ref, kp_ref, kc_ref, kn_ref, vp_ref, vc_ref, vn_ref, o_ref):
    i = pl.program_id(1)
    nblk = pl.num_programs(1)
    rows2 = 2 * BLOCK
    r = lax.broadcasted_iota(jnp.int32, (rows2, BLOCK), 0) % BLOCK
    c = lax.broadcasted_iota(jnp.int32, (rows2, BLOCK), 1)
    top = lax.broadcasted_iota(jnp.int32, (rows2, 1), 0) < BLOCK
    neg = jnp.float32(-jnp.inf)
    bias_p = jnp.where((c >= r) & (i > 0), 0.0, neg)
    bias_n = jnp.where((c <= r) & (i < nblk - 1), 0.0, neg)
    lane3 = lax.broadcasted_iota(jnp.int32, (3 * BLOCK, LANES), 1)
    low3 = lane3 < HEAD_DIM
    low2 = c < HEAD_DIM

    def padded_variants(p_ref, c_ref, n_ref, slab):
        sl = slice(slab * LANES, (slab + 1) * LANES)
        x = jnp.concatenate([p_ref[:, sl], c_ref[:, sl], n_ref[:, sl]], axis=0)
        xr = pltpu.bitcast(pltpu.roll(pltpu.bitcast(x, jnp.uint32), HEAD_DIM, 1), BF16)
        zero = jnp.zeros_like(x)
        return ((jnp.where(low3, x, zero), jnp.where(low3, zero, xr)),
                (jnp.where(low3, xr, zero), jnp.where(low3, zero, x)))

    for slab in range(N_KV_HEADS // 2):
        k_var = padded_variants(kp_ref, kc_ref, kn_ref, slab)
        v_var = padded_variants(vp_ref, vc_ref, vn_ref, slab)
        for par in range(2):
            g = 2 * slab + par
            qq = jnp.concatenate([q_ref[:, (2 * g) * LANES:(2 * g + 1) * LANES],
                                  q_ref[:, (2 * g + 1) * LANES:(2 * g + 2) * LANES]], axis=0)
            kk = jnp.concatenate(k_var[par], axis=0)
            vv = jnp.concatenate(v_var[par], axis=0)
            s = lax.dot_general(qq, kk, (((1,), (1,)), ((), ())), preferred_element_type=F32)
            probs, inv = [], []
            for e in range(2):
                base = e * 3 * BLOCK
                sp = s[:, base:base + BLOCK] + bias_p
                sc = s[:, base + BLOCK:base + 2 * BLOCK]
                sn = s[:, base + 2 * BLOCK:base + 3 * BLOCK] + bias_n
                sink = jnp.where(top, sink_ref[4 * g + e], sink_ref[4 * g + 2 + e])
                m = jnp.maximum(jnp.max(jnp.maximum(jnp.maximum(sp, sc), sn), axis=-1, keepdims=True), sink)
                pp, pc, pn = jnp.exp(sp - m), jnp.exp(sc - m), jnp.exp(sn - m)
                denom = jnp.sum(pp + pc + pn, axis=-1, keepdims=True) + jnp.exp(sink - m)
                probs += [pp.astype(BF16), pc.astype(BF16), pn.astype(BF16)]
                inv.append(1.0 / denom)
            o = jnp.dot(jnp.concatenate(probs, axis=1), vv, preferred_element_type=F32)
            o = o * jnp.where(low2, inv[0], inv[1])
            o_ref[:, (2 * g) * LANES:(2 * g + 1) * LANES] = o[:BLOCK].astype(BF16)
            o_ref[:, (2 * g + 1) * LANES:(2 * g + 2) * LANES] = o[BLOCK:].astype(BF16)


def _attention(proj3, sinks):
    batch, seq, _ = proj3.shape
    nblk = seq // BLOCK
    kcol, vcol = COL_K // KV_WIDTH, COL_V // KV_WIDTH
    prev = lambda col: pl.BlockSpec((None, BLOCK, KV_WIDTH), lambda b, i: (b, jnp.maximum(i - 1, 0), col))
    cur = lambda col: pl.BlockSpec((None, BLOCK, KV_WIDTH), lambda b, i: (b, i, col))
    nxt = lambda col: pl.BlockSpec((None, BLOCK, KV_WIDTH), lambda b, i: (b, jnp.minimum(i + 1, nblk - 1), col))
    return pl.pallas_call(
        _attn_kernel,
        grid=(batch, nblk),
        in_specs=[pl.BlockSpec(memory_space=pltpu.SMEM),
                  pl.BlockSpec((None, BLOCK, D_MODEL), lambda b, i: (b, i, COL_Q // D_MODEL)),
                  prev(kcol), cur(kcol), nxt(kcol), prev(vcol), cur(vcol), nxt(vcol)],
        out_specs=pl.BlockSpec((None, BLOCK, D_MODEL), lambda b, i: (b, i, 0)),
        out_shape=jax.ShapeDtypeStruct((batch, seq, D_MODEL), BF16),
        compiler_params=_cparams(("arbitrary", "arbitrary")),
        name="window_attn",
    )(sinks, proj3, proj3, proj3, proj3, proj3, proj3, proj3)


CONV_ROWS = 256
N_SLABS = CONV_DIM // LANES


def _conv_kernel(xp_ref, xm_ref, xn_ref, cw_ref, o_ref, in_scr, out_scr):
    i = pl.program_id(1)
    n = pl.num_programs(1)
    has_prev = (i > 0).astype(F32)
    has_next = (i < n - 1).astype(F32)
    rows = CONV_ROWS
    for cs in range(N_SLABS):
        sl = slice(cs * LANES, (cs + 1) * LANES)
        in_scr[cs, 0:HALO, :] = xp_ref[:, sl].astype(F32) * has_prev
        in_scr[cs, HALO:HALO + rows, :] = xm_ref[:, sl].astype(F32)
        in_scr[cs, HALO + rows:, :] = xn_ref[:, sl].astype(F32) * has_next

    def slab_body(cs, carry):
        cw = cw_ref[cs]
        taps = [jnp.broadcast_to(cw[w:w + 1, :], (8, LANES)) for w in range(CONV_W)]
        bias = jnp.broadcast_to(cw[CONV_W:CONV_W + 1, :], (8, LANES))
        for g in range(rows // 16):
            for e in range(2):
                acc = bias
                for w in range(CONV_W):
                    start = HALO + 16 * g + e + w - CONV_W // 2
                    acc = acc + in_scr[cs, pl.ds(start, 8, stride=2), :] * taps[w]
                out_scr[cs, pl.ds(16 * g + e, 8, stride=2), :] = _silu(acc)
        return carry

    lax.fori_loop(0, N_SLABS, slab_body, 0)
    for cs in range(N_SLABS):
        o_ref[:, cs * LANES:(cs + 1) * LANES] = out_scr[cs].astype(BF16)


def _conv_silu(proj3, conv_taps):
    batch, seq, _ = proj3.shape
    rows = CONV_ROWS
    hpr = rows // HALO
    nh = seq // HALO
    col = COL_XBC // CONV_DIM
    return pl.pallas_call(
        _conv_kernel,
        grid=(batch, seq // rows),
        in_specs=[pl.BlockSpec((None, HALO, CONV_DIM), lambda b, i: (b, jnp.maximum(i * hpr - 1, 0), col)),
                  pl.BlockSpec((None, rows, CONV_DIM), lambda b, i: (b, i, col)),
                  pl.BlockSpec((None, HALO, CONV_DIM), lambda b, i: (b, jnp.minimum((i + 1) * hpr, nh - 1), col)),
                  pl.BlockSpec((N_SLABS, 8, LANES), lambda b, i: (0, 0, 0))],
        out_specs=pl.BlockSpec((None, rows, CONV_DIM), lambda b, i: (b, i, 0)),
        out_shape=jax.ShapeDtypeStruct((batch, seq, CONV_DIM), BF16),
        scratch_shapes=[pltpu.VMEM((N_SLABS, rows + 2 * HALO, LANES), F32),
                        pltpu.VMEM((N_SLABS, rows, LANES), F32)],
        compiler_params=_cparams(("arbitrary", "arbitrary")),
        name="conv_silu",
    )(proj3, proj3, proj3, conv_taps)


def _ssd_direction(xa_ref, dt_ref, bias_ref, a_ref, e_ref, dskip_ref, st_ref, out_ref, *, reverse):
    lane0 = SSD_HEADS if reverse else 0
    xs = xa_ref[:, :D_INNER].astype(F32)
    bm = xa_ref[:, D_INNER:D_INNER + N_GROUPS * D_STATE]
    cm = xa_ref[:, D_INNER + N_GROUPS * D_STATE:]

    dt = jax.nn.softplus(dt_ref[...] + bias_ref[...])
    da = dt * a_ref[...]
    r = lax.broadcasted_iota(jnp.int32, (CHUNK, CHUNK), 0)
    c = lax.broadcasted_iota(jnp.int32, (CHUNK, CHUNK), 1)
    causal = (c >= r) if reverse else (c <= r)
    tri = causal.astype(BF16)
    d1 = da.astype(BF16)
    r1 = da - d1.astype(F32)
    d2 = r1.astype(BF16)
    d3 = (r1 - d2.astype(F32)).astype(BF16)
    cum = (jnp.dot(tri, d1, preferred_element_type=F32) + jnp.dot(tri, d2, preferred_element_type=F32)
           + jnp.dot(tri, d3, preferred_element_type=F32))
    src_t = (cum - jnp.log(dt)).T
    edge = CHUNK - 1 if not reverse else 0
    total = cum[edge:edge + 1, :]
    exp_cum = jnp.exp(cum)
    decay = jnp.exp(total - cum)

    e = e_ref[...]

    def expand(v):
        return jnp.dot(v.astype(BF16), e, preferred_element_type=F32)

    x_dec = (xs * expand(dt * decay)).astype(BF16)
    exp_cum_e = expand(exp_cum)
    tot8 = jnp.broadcast_to(jnp.exp(total), (8, DT_W))
    tot_hi = tot8.astype(BF16)
    tot_lo = (tot8 - tot_hi.astype(F32)).astype(BF16)
    chunk_decay = (jnp.dot(tot_hi, e, preferred_element_type=F32)
                   + jnp.dot(tot_lo, e, preferred_element_type=F32))[0:1, :]

    heads_per_group = SSD_HEADS // N_GROUPS
    gw = heads_per_group * SSD_HEAD_DIM
    lane = lax.broadcasted_iota(jnp.int32, (CHUNK, LANES), 1)
    neg = jnp.float32(-jnp.inf)
    for g in range(N_GROUPS):
        b_g = bm[:, g * D_STATE:(g + 1) * D_STATE]
        c_g = cm[:, g * D_STATE:(g + 1) * D_STATE]
        cb = lax.dot_general(c_g, b_g, (((1,), (1,)), ((), ())), preferred_element_type=F32)
        gs = slice(g * gw, (g + 1) * gw)
        st_prev = st_ref[:, gs]
        y = jnp.dot(c_g, st_prev.astype(BF16), preferred_element_type=F32) * exp_cum_e[:, gs]
        y_diag = []
        for pair in range(heads_per_group // 2):
            ms = []
            for hh in range(2):
                k = lane0 + g * heads_per_group + pair * 2 + hh
                seg = cum[:, k:k + 1] - src_t[k:k + 1, :]
                ms.append((cb * jnp.exp(jnp.where(causal, seg, neg))).astype(BF16))
            col = g * gw + pair * LANES
            xp = xa_ref[:, col:col + LANES]
            zero = jnp.zeros_like(xp)
            rhs = jnp.concatenate([jnp.where(lane < SSD_HEAD_DIM, xp, zero),
                                   jnp.where(lane >= SSD_HEAD_DIM, xp, zero)], axis=0)
            y_diag.append(jnp.dot(jnp.concatenate(ms, axis=1), rhs, preferred_element_type=F32))
        y = y + jnp.concatenate(y_diag, axis=1)
        if dskip_ref is not None:
            y = y + dskip_ref[:, gs] * xs[:, gs]
        out_ref[:, gs] = y.astype(out_ref.dtype)
        upd = jnp.dot(b_g.astype(F32).T.astype(BF16), x_dec[:, gs], preferred_element_type=F32)
        st_ref[:, gs] = st_prev * chunk_decay[:, gs] + upd


def _ssd_kernel(xa_f, dt_f, xa_b, dt_b, bias_ref, a_ref, e_f, e_b, dskip_ref, yf_ref, yb_ref, st_f, st_b):
    @pl.when(pl.program_id(1) == 0)
    def _():
        st_f[...] = jnp.zeros_like(st_f)
        st_b[...] = jnp.zeros_like(st_b)

    _ssd_direction(xa_f, dt_f, bias_ref, a_ref, e_f, dskip_ref, st_f, yf_ref, reverse=False)
    _ssd_direction(xa_b, dt_b, bias_ref, a_ref, e_b, None, st_b, yb_ref, reverse=True)


def _ssd_scan(xact3, dt3, dt_bias, a_neg, e_f, e_b, dskip_e):
    batch, seq, _ = xact3.shape
    nc = seq // CHUNK

    def specs(chunk_of):
        return [pl.BlockSpec((None, CHUNK, CONV_DIM), lambda b, c: (b, chunk_of(c), 0)),
                pl.BlockSpec((None, CHUNK, DT_W), lambda b, c: (b, chunk_of(c), 0))]

    fwd = lambda c: c
    bwd = lambda c: nc - 1 - c
    const = lambda shape: pl.BlockSpec(shape, lambda b, c: (0,) * len(shape))
    out_f = pl.BlockSpec((None, CHUNK, D_INNER), lambda b, c: (b, c, 0))
    out_b = pl.BlockSpec((None, CHUNK, D_INNER), lambda b, c: (b, nc - 1 - c, 0))
    y_shape = jax.ShapeDtypeStruct((batch, seq, D_INNER), BF16)
    return pl.pallas_call(
        _ssd_kernel,
        grid=(batch, nc),
        in_specs=specs(fwd) + specs(bwd) + [
            const((1, DT_W)), const((1, DT_W)),
            const((DT_W, D_INNER)), const((DT_W, D_INNER)), const((1, D_INNER))],
        out_specs=[out_f, out_b],
        out_shape=[y_shape, y_shape],
        scratch_shapes=[pltpu.VMEM((D_STATE, D_INNER), F32), pltpu.VMEM((D_STATE, D_INNER), F32)],
        compiler_params=_cparams(("arbitrary", "arbitrary")),
        name="ssd_scan",
    )(xact3, dt3, xact3, dt3, dt_bias, a_neg, e_f, e_b, dskip_e)


def _outproj_kernel(x_ref, gt_ref, yf_ref, yb_ref, z_ref, attn_ref, ga_ref, gs_ref, nw_ref, w_ref, gpost_ref,
                    o_ref):
    y = (yf_ref[...].astype(F32) + yb_ref[...].astype(F32)) * _silu(z_ref[...].astype(F32))
    gw = D_INNER // N_GROUPS
    parts = []
    for g in range(N_GROUPS):
        yg = y[:, g * gw:(g + 1) * gw]
        parts.append(yg * lax.rsqrt(jnp.mean(yg * yg, axis=-1, keepdims=True) + EPS))
    ssd = jnp.concatenate(parts, axis=1) * nw_ref[...]
    merged = (jax.nn.sigmoid(ga_ref[...].astype(F32)) * attn_ref[...].astype(F32)
              + jax.nn.sigmoid(gs_ref[...].astype(F32)) * ssd)
    mix = jnp.dot(merged.astype(BF16), w_ref[...], preferred_element_type=F32)
    nrm = mix * lax.rsqrt(jnp.mean(mix * mix, axis=-1, keepdims=True) + EPS) * gpost_ref[...]
    o_ref[...] = x_ref[...] + gt_ref[...] * nrm


def _out_projection(x2, mod4, yf2, yb2, proj2, attn2, ssd_norm_w, w_out, g_post, batch, seq):
    t_rows = batch * seq
    tm = min(seq, 256)
    per_b = seq // tm
    row = lambda col: pl.BlockSpec((tm, D_MODEL), lambda i: (i, col))
    vec = pl.BlockSpec((1, D_MODEL), lambda i: (0, 0))
    return pl.pallas_call(
        _outproj_kernel,
        grid=(t_rows // tm,),
        in_specs=[row(0),
                  pl.BlockSpec((None, None, 1, D_MODEL), lambda i: (i // per_b, 2, 0, 0)),
                  row(0), row(0), row(COL_Z // D_MODEL), row(0),
                  row(COL_GATE // D_MODEL), row(COL_GATE // D_MODEL + 1),
                  vec,
                  pl.BlockSpec((D_MODEL, D_MODEL), lambda i: (0, 0)),
                  vec],
        out_specs=row(0),
        out_shape=jax.ShapeDtypeStruct((t_rows, D_MODEL), F32),
        compiler_params=_cparams(("arbitrary",)),
        name="merge_out_proj",
    )(x2, mod4, yf2, yb2, proj2, attn2, proj2, proj2, ssd_norm_w.reshape(1, D_INNER), w_out,
      g_post.reshape(1, D_MODEL))


def _ffn_kernel(x_ref, sc_ref, sh_ref, gt_ref, gpre_ref, wg_ref, wu_ref, wd_ref, gpost_ref, o_ref, h_scr, acc_scr):
    j = pl.program_id(1)

    @pl.when(j == 0)
    def _():
        x = x_ref[...]
        ms = jnp.mean(x * x, axis=-1, keepdims=True)
        y = x * lax.rsqrt(ms + EPS) * gpre_ref[...]
        h_scr[...] = (y * (1.0 + sc_ref[...]) + sh_ref[...]).astype(BF16)
        acc_scr[...] = jnp.zeros_like(acc_scr)

    h = h_scr[...]
    gate = jnp.dot(h, wg_ref[...], preferred_element_type=F32)
    up = jnp.dot(h, wu_ref[...], preferred_element_type=F32)
    act = (_silu(gate) * up).astype(BF16)
    acc_scr[...] += jnp.dot(act, wd_ref[...], preferred_element_type=F32)

    @pl.when(j == pl.num_programs(1) - 1)
    def _():
        f = acc_scr[...]
        nrm = f * lax.rsqrt(jnp.mean(f * f, axis=-1, keepdims=True) + EPS) * gpost_ref[...]
        o_ref[...] = x_ref[...] + gt_ref[...] * nrm


def _ffn(x2, mod4, g_pre, w_g, w_u, w_d, g_post, batch, seq):
    t_rows = batch * seq
    tm = min(seq, 512)
    tf = 512
    per_b = seq // tm
    mod_spec = lambda k: pl.BlockSpec((None, None, 1, D_MODEL), lambda i, j: (i // per_b, k, 0, 0))
    vec = pl.BlockSpec((1, D_MODEL), lambda i, j: (0, 0))
    return pl.pallas_call(
        _ffn_kernel,
        grid=(t_rows // tm, D_FF // tf),
        in_specs=[pl.BlockSpec((tm, D_MODEL), lambda i, j: (i, 0)),
                  mod_spec(4), mod_spec(3), mod_spec(5), vec,
                  pl.BlockSpec((D_MODEL, tf), lambda i, j: (0, j)),
                  pl.BlockSpec((D_MODEL, tf), lambda i, j: (0, j)),
                  pl.BlockSpec((tf, D_MODEL), lambda i, j: (j, 0)),
                  vec],
        out_specs=pl.BlockSpec((tm, D_MODEL), lambda i, j: (i, 0)),
        out_shape=jax.ShapeDtypeStruct((t_rows, D_MODEL), F32),
        scratch_shapes=[pltpu.VMEM((tm, D_MODEL), BF16), pltpu.VMEM((tm, D_MODEL), F32)],
        compiler_params=_cparams(("arbitrary", "arbitrary")),
        name="swiglu_ffn",
    )(x2, mod4, mod4, mod4, g_pre.reshape(1, D_MODEL), w_g, w_u, w_d, g_post.reshape(1, D_MODEL))


def _prep_weights(w_in, conv_w, conv_b, a_log_f, a_log_b, dt_bias_f, dt_bias_b, d_skip, w_out, w_gu, w_down):
    o_q, o_k, o_v = 0, D_MODEL, D_MODEL + KV_WIDTH
    o_z = o_v + KV_WIDTH
    o_xbc = o_z + D_INNER
    o_dt = o_xbc + CONV_DIM
    o_gate = o_dt + 2 * SSD_HEADS
    seg = lambda off, width: w_in[:, off:off + width]
    w_main = jnp.concatenate([seg(o_xbc, CONV_DIM), seg(o_gate, 2 * D_MODEL), seg(o_q, D_MODEL),
                              seg(o_z, D_INNER), seg(o_k, KV_WIDTH), seg(o_v, KV_WIDTH)], axis=1).astype(BF16)
    w_dt = jnp.pad(seg(o_dt, 2 * SSD_HEADS), ((0, 0), (0, DT_W - 2 * SSD_HEADS))).astype(BF16)
    pad_l = lambda v: jnp.pad(v, (0, DT_W - v.shape[0])).reshape(1, DT_W)
    dt_bias = pad_l(jnp.concatenate([dt_bias_f, dt_bias_b]))
    a_neg = pad_l(-jnp.exp(jnp.concatenate([a_log_f, a_log_b])))
    taps8 = jnp.concatenate([conv_w, conv_b[None, :], jnp.zeros((8 - CONV_W - 1, CONV_DIM), F32)], axis=0)
    conv_taps = taps8.reshape(8, N_SLABS, LANES).transpose(1, 0, 2)
    head_of_lane = jnp.arange(D_INNER) // SSD_HEAD_DIM
    rows = jnp.arange(DT_W)[:, None]
    e_f = (rows == head_of_lane[None, :]).astype(BF16)
    e_b = (rows == head_of_lane[None, :] + SSD_HEADS).astype(BF16)
    dskip_e = jnp.repeat(d_skip, SSD_HEAD_DIM).reshape(1, D_INNER)
    return dict(w_main=w_main, w_dt=w_dt, dt_bias=dt_bias, a_neg=a_neg, conv_taps=conv_taps,
                e_f=e_f, e_b=e_b, dskip_e=dskip_e,
                w_out=w_out.astype(BF16), w_g=w_gu[:, :D_FF].astype(BF16), w_u=w_gu[:, D_FF:].astype(BF16),
                w_d=w_down.astype(BF16))


def _encoder_layer(x, mod4, tables, pw, g_pre1, g_post1, ssd_norm_w, sinks, g_pre2, g_post2):
    batch, seq, _ = x.shape
    x2 = x.reshape(batch * seq, D_MODEL)
    proj2, dt2 = _in_projection(x2, mod4, g_pre1, pw["w_main"], pw["w_dt"], tables, batch, seq)
    proj3 = proj2.reshape(batch, seq, PROJ_W)
    dt3 = dt2.reshape(batch, seq, DT_W)
    attn = _attention(proj3, sinks)
    xact3 = _conv_silu(proj3, pw["conv_taps"])
    yf, yb = _ssd_scan(xact3, dt3, pw["dt_bias"], pw["a_neg"], pw["e_f"], pw["e_b"], pw["dskip_e"])
    x1 = _out_projection(x2, mod4, yf.reshape(batch * seq, D_INNER), yb.reshape(batch * seq, D_INNER), proj2,
                         attn.reshape(batch * seq, D_MODEL), ssd_norm_w, pw["w_out"], g_post1, batch, seq)
    out = _ffn(x1, mod4, g_pre2, pw["w_g"], pw["w_u"], pw["w_d"], g_post2, batch, seq)
    return out.reshape(batch, seq, D_MODEL)


def kernel(x_prompt, x_sample, c_prompt, c_sample, w_ada, b_ada, g_pre1, g_post1, w_in, conv_w, conv_b, a_log_f,
           a_log_b, dt_bias_f, dt_bias_b, d_skip, ssd_norm_w, sinks, w_out, g_pre2, g_post2, w_gu, w_down):
    assert w_ada.shape[0] == 1, "single-layer problem"
    nb_p, nb_s = c_prompt.shape[0], c_sample.shape[0]
    nb = -(-(nb_p + nb_s) // 8) * 8
    c_all = jnp.pad(jnp.concatenate([c_prompt, c_sample], axis=0), ((0, nb - nb_p - nb_s), (0, 0)))
    mod = _modulation(c_all, w_ada[0], b_ada[0]).reshape(nb, 6, 1, D_MODEL)
    pw = _prep_weights(w_in[0], conv_w[0], conv_b[0], a_log_f[0], a_log_b[0], dt_bias_f[0], dt_bias_b[0],
                       d_skip[0], w_out[0], w_gu[0], w_down[0])
    outs = []
    for x, mod4 in ((x_prompt, mod[:nb_p]), (x_sample, mod[nb_p:nb_p + nb_s])):
        tables = _rope_tables(x.shape[1])
        outs.append(_encoder_layer(x, mod4, tables, pw, g_pre1[0], g_post1[0], ssd_norm_w[0], sinks[0],
                                   g_pre2[0], g_post2[0]))
    return tuple(outs)
```

```python
import functools
import math

import jax
import jax.numpy as jnp
from jax import lax
from jax.experimental import pallas as pl
from jax.experimental.pallas import tpu as pltpu

F32 = jnp.float32
BF16 = jnp.bfloat16

D_MODEL = 2048
N_HEADS = 32
N_KV_HEADS = 8
HEAD_DIM = 64
KV_WIDTH = N_KV_HEADS * HEAD_DIM
BLOCK = 128
ROPE_THETA = 500000.0
ROPE_DIM = HEAD_DIM // 4
SSD_HEADS = 32
SSD_HEAD_DIM = 64
D_INNER = SSD_HEADS * SSD_HEAD_DIM
D_STATE = 128
N_GROUPS = 8
CONV_W = 5
CONV_DIM = D_INNER + 2 * N_GROUPS * D_STATE
CHUNK = 128
D_FF = 5632
EPS = 1e-6
LOG2E = math.log2(math.e)

LANES = 128
HALO = 16

COL_XBC = 0
COL_GATE = COL_XBC + CONV_DIM
COL_Q = COL_GATE + 2 * D_MODEL
COL_Z = COL_Q + D_MODEL
COL_K = COL_Z + D_INNER
COL_V = COL_K + KV_WIDTH
PROJ_W = COL_V + KV_WIDTH
DT_W = LANES

VMEM_LIMIT = 56 * 1024 * 1024


def _cparams(sem):
    return pltpu.CompilerParams(dimension_semantics=sem, vmem_limit_bytes=VMEM_LIMIT)


def _silu(v):
    return v * jax.nn.sigmoid(v)


NORM_ROWS = 16


def _rowwise_rms_apply(src_ref, apply):
    n_rows, width = src_ref.shape
    v = src_ref[...]
    inv = lax.rsqrt(jnp.sum(v * v, axis=-1, keepdims=True) * (1.0 / width) + EPS)
    for r in range(n_rows // NORM_ROWS):
        rows = slice(r * NORM_ROWS, (r + 1) * NORM_ROWS)
        apply(rows, src_ref[rows, :], inv[rows])


def _modulated_prenorm(x_ref, g_ref, sc_ref, sh_ref, h_ref):
    tile = (NORM_ROWS, x_ref.shape[1])
    gmod = jnp.broadcast_to(g_ref[...] * (1.0 + sc_ref[...]), tile)
    shift = jnp.broadcast_to(sh_ref[...], tile)

    def apply(rows, x, inv):
        h_ref[rows, :] = (x * inv * gmod + shift).astype(BF16)

    _rowwise_rms_apply(x_ref, apply)


def _gated_postnorm_residual(f_ref, x_ref, gpost_ref, gt_ref, o_ref):
    gg = jnp.broadcast_to(gpost_ref[...] * gt_ref[...], (NORM_ROWS, f_ref.shape[1]))

    def apply(rows, f, inv):
        o_ref[rows, :] = x_ref[rows, :] + f * inv * gg

    _rowwise_rms_apply(f_ref, apply)


def _mod_kernel(c_ref, w_ref, b_ref, o_ref):
    c = c_ref[...]
    s = _silu(c).astype(BF16)
    o_ref[...] = jnp.dot(s, w_ref[...].astype(BF16), preferred_element_type=F32) + b_ref[...]


def _modulation(c_all, w_ada, b_ada):
    nb = c_all.shape[0]
    n = w_ada.shape[1]
    tn = 1024
    return pl.pallas_call(
        _mod_kernel,
        grid=(n // tn,),
        in_specs=[pl.BlockSpec((nb, D_MODEL), lambda j: (0, 0)),
                  pl.BlockSpec((D_MODEL, tn), lambda j: (0, j)),
                  pl.BlockSpec((1, tn), lambda j: (0, j))],
        out_specs=pl.BlockSpec((nb, tn), lambda j: (0, j)),
        out_shape=jax.ShapeDtypeStruct((nb, n), F32),
        compiler_params=_cparams(("arbitrary",)),
        name="adaln_mod",
    )(c_all, w_ada, b_ada.reshape(1, n))


def _rope_table_kernel(cos_ref, sin_lo_ref, sin_hi_ref):
    tm = cos_ref.shape[0]
    half = ROPE_DIM // 2
    pos = (pl.program_id(0) * tm + lax.broadcasted_iota(jnp.int32, (tm, LANES), 0)).astype(F32)
    d = lax.broadcasted_iota(jnp.int32, (tm, LANES), 1) % HEAD_DIM
    fidx = (d % half).astype(F32)
    inv = jnp.exp(fidx * (-2.0 / ROPE_DIM * math.log(ROPE_THETA)))
    ang = pos * inv
    cos = jnp.cos(ang)
    sin = jnp.sin(ang)
    cos_ref[...] = jnp.where(d < ROPE_DIM, cos, 1.0)
    sin_lo_ref[...] = jnp.where(d < half, -sin, 0.0)
    sin_hi_ref[...] = jnp.where((d >= half) & (d < ROPE_DIM), sin, 0.0)


def _rope_tables(seq):
    tm = min(seq, 2048)
    shp = jax.ShapeDtypeStruct((seq, LANES), F32)
    spec = pl.BlockSpec((tm, LANES), lambda i: (i, 0))
    return pl.pallas_call(
        _rope_table_kernel,
        grid=(seq // tm,),
        in_specs=[],
        out_specs=[spec, spec, spec],
        out_shape=[shp, shp, shp],
        compiler_params=_cparams(("arbitrary",)),
        name="rope_tables",
    )()


def _rope_slab(t, cos, sin_lo, sin_hi):
    half = ROPE_DIM // 2
    return t * cos + pltpu.roll(t, LANES - half, 1) * sin_lo + pltpu.roll(t, half, 1) * sin_hi


def _inproj_kernel(x_ref, sc_ref, sh_ref, g_ref, w_ref, wdt_ref, cos_ref, slo_ref, shi_ref,
                   proj_ref, dt_ref, h_scr, *, tn):
    j = pl.program_id(1)

    @pl.when(j == 0)
    def _():
        _modulated_prenorm(x_ref, g_ref, sc_ref, sh_ref, h_scr)
        dt_ref[...] = jnp.dot(h_scr[...], wdt_ref[...], preferred_element_type=F32)

    q_lo, q_hi = COL_Q // tn, (COL_Q + D_MODEL) // tn
    k_tile = COL_K // tn
    k_slabs = KV_WIDTH // LANES
    is_q = (j >= q_lo) & (j < q_hi)
    is_k = j == k_tile
    tm = h_scr.shape[0]
    n_parts = 2
    rows = tm // n_parts

    def project(n_rope_slabs, scale):
        for p in range(n_parts):
            rs = slice(p * rows, (p + 1) * rows)
            acc = jnp.dot(h_scr[rs, :], w_ref[...], preferred_element_type=F32)
            if n_rope_slabs == 0:
                proj_ref[rs, :] = acc.astype(BF16)
                continue
            cos, slo, shi = cos_ref[rs, :], slo_ref[rs, :], shi_ref[rs, :]
            for s in range(tn // LANES):
                t = acc[:, s * LANES:(s + 1) * LANES]
                if s < n_rope_slabs:
                    t = _rope_slab(t, cos, slo, shi)
                    if scale != 1.0:
                        t = t * scale
                proj_ref[rs, s * LANES:(s + 1) * LANES] = t.astype(BF16)

    @pl.when(is_q)
    def _():
        project(tn // LANES, HEAD_DIM ** -0.5 * LOG2E)

    @pl.when(is_k)
    def _():
        project(k_slabs, 1.0)

    @pl.when(jnp.logical_not(is_q | is_k))
    def _():
        project(0, 1.0)


def _in_projection(x2, mod4, g_pre, w_main, w_dt, tables, batch, seq):
    t_rows = batch * seq
    tm = min(seq, 1024)
    tn = 1024
    assert COL_K % tn == 0 and COL_Q % tn == 0 and D_MODEL % tn == 0 and PROJ_W % tn == 0
    per_b = seq // tm
    cos, slo, shi = tables
    tab_spec = pl.BlockSpec((tm, LANES), lambda i, j: (i % per_b, 0))
    mod_spec = lambda k: pl.BlockSpec((None, None, 1, D_MODEL), lambda i, j: (i // per_b, k, 0, 0))
    return pl.pallas_call(
        functools.partial(_inproj_kernel, tn=tn),
        grid=(t_rows // tm, PROJ_W // tn),
        in_specs=[pl.BlockSpec((tm, D_MODEL), lambda i, j: (i, 0)),
                  mod_spec(1), mod_spec(0),
                  pl.BlockSpec((1, D_MODEL), lambda i, j: (0, 0)),
                  pl.BlockSpec((D_MODEL, tn), lambda i, j: (0, j)),
                  pl.BlockSpec((D_MODEL, DT_W), lambda i, j: (0, 0)),
                  tab_spec, tab_spec, tab_spec],
        out_specs=[pl.BlockSpec((tm, tn), lambda i, j: (i, j)),
                   pl.BlockSpec((tm, DT_W), lambda i, j: (i, 0))],
        out_shape=[jax.ShapeDtypeStruct((t_rows, PROJ_W), BF16),
                   jax.ShapeDtypeStruct((t_rows, DT_W), F32)],
        scratch_shapes=[pltpu.VMEM((tm, D_MODEL), BF16)],
        compiler_params=_cparams(("arbitrary", "arbitrary")),
        name="norm_in_proj",
    )(x2, mod4, mod4, g_pre.reshape(1, D_MODEL), w_main, w_dt, cos, slo, shi)


def _attn_kernel(sink_ref, q_ref, kp_ref, kc_ref, kn_ref, vp_ref, vc_ref, vn_ref, o_ref):
    i = pl.program_id(1)
    nblk = pl.num_programs(1)
    rows2 = 2 * BLOCK
    r = lax.broadcasted_iota(jnp.int32, (rows2, BLOCK), 0) % BLOCK
    c = lax.broadcasted_iota(jnp.int32, (rows2, BLOCK), 1)
    top = lax.broadcasted_iota(jnp.int32, (rows2, 1), 0) < BLOCK
    neg = jnp.float32(-jnp.inf)
    bias_p = jnp.where((c >= r) & (i > 0), 0.0, neg)
    bias_n = jnp.where((c <= r) & (i < nblk - 1), 0.0, neg)
    lane3 = lax.broadcasted_iota(jnp.int32, (3 * BLOCK, LANES), 1)
    low3 = lane3 < HEAD_DIM
    low2 = c < HEAD_DIM

    def padded_variants(p_ref, c_ref, n_ref, slab):
        sl = slice(slab * LANES, (slab + 1) * LANES)
        x = jnp.concatenate([p_ref[:, sl], c_ref[:, sl], n_ref[:, sl]], axis=0)
        xr = pltpu.bitcast(pltpu.roll(pltpu.bitcast(x, jnp.uint32), HEAD_DIM, 1), BF16)
        zero = jnp.zeros_like(x)
        return ((jnp.where(low3, x, zero), jnp.where(low3, zero, xr)),
                (jnp.where(low3, xr, zero), jnp.where(low3, zero, x)))

    variants = {}

    def scores(g):
        slab, par = divmod(g, 2)
        if slab not in variants:
            variants[slab] = (padded_variants(kp_ref, kc_ref, kn_ref, slab),
                              padded_variants(vp_ref, vc_ref, vn_ref, slab))
        k_var, v_var = variants[slab]
        qq = jnp.concatenate([q_ref[:, (2 * g) * LANES:(2 * g + 1) * LANES],
                              q_ref[:, (2 * g + 1) * LANES:(2 * g + 2) * LANES]], axis=0)
        kk = jnp.concatenate(k_var[par], axis=0)
        s = lax.dot_general(qq, kk, (((1,), (1,)), ((), ())), preferred_element_type=F32)
        return s, jnp.concatenate(v_var[par], axis=0)

    def finish(g, s, vv):
        probs, inv = [], []
        for e in range(2):
            base = e * 3 * BLOCK
            sp = s[:, base:base + BLOCK] + bias_p
            sc = s[:, base + BLOCK:base + 2 * BLOCK]
            sn = s[:, base + 2 * BLOCK:base + 3 * BLOCK] + bias_n
            sink = jnp.where(top, sink_ref[4 * g + e], sink_ref[4 * g + 2 + e]) * LOG2E
            m = jnp.maximum(jnp.max(jnp.maximum(jnp.maximum(sp, sc), sn), axis=-1, keepdims=True), sink)
            pp, pc, pn = jnp.exp2(sp - m), jnp.exp2(sc - m), jnp.exp2(sn - m)
            denom = jnp.sum(pp + pc + pn, axis=-1, keepdims=True) + jnp.exp2(sink - m)
            probs += [pp.astype(BF16), pc.astype(BF16), pn.astype(BF16)]
            inv.append(1.0 / denom)
        o = jnp.dot(jnp.concatenate(probs, axis=1), vv, preferred_element_type=F32)
        o = o * jnp.where(low2, inv[0], inv[1])
        o_ref[:, (2 * g) * LANES:(2 * g + 1) * LANES] = o[:BLOCK].astype(BF16)
        o_ref[:, (2 * g + 1) * LANES:(2 * g + 2) * LANES] = o[BLOCK:].astype(BF16)

    pending = scores(0)
    for g in range(N_KV_HEADS):
        nxt = scores(g + 1) if g + 1 < N_KV_HEADS else None
        finish(g, *pending)
        pending = nxt


def _attention(proj3, sinks):
    batch, seq, _ = proj3.shape
    nblk = seq // BLOCK
    kcol, vcol = COL_K // KV_WIDTH, COL_V // KV_WIDTH
    prev = lambda col: pl.BlockSpec((None, BLOCK, KV_WIDTH), lambda b, i: (b, jnp.maximum(i - 1, 0), col))
    cur = lambda col: pl.BlockSpec((None, BLOCK, KV_WIDTH), lambda b, i: (b, i, col))
    nxt = lambda col: pl.BlockSpec((None, BLOCK, KV_WIDTH), lambda b, i: (b, jnp.minimum(i + 1, nblk - 1), col))
    return pl.pallas_call(
        _attn_kernel,
        grid=(batch, nblk),
        in_specs=[pl.BlockSpec(memory_space=pltpu.SMEM),
                  pl.BlockSpec((None, BLOCK, D_MODEL), lambda b, i: (b, i, COL_Q // D_MODEL)),
                  prev(kcol), cur(kcol), nxt(kcol), prev(vcol), cur(vcol), nxt(vcol)],
        out_specs=pl.BlockSpec((None, BLOCK, D_MODEL), lambda b, i: (b, i, 0)),
        out_shape=jax.ShapeDtypeStruct((batch, seq, D_MODEL), BF16),
        compiler_params=_cparams(("arbitrary", "arbitrary")),
        name="window_attn",
    )(sinks, proj3, proj3, proj3, proj3, proj3, proj3, proj3)


CONV_ROWS = 256
N_SLABS = CONV_DIM // LANES


def _conv_kernel(xp_ref, xm_ref, xn_ref, cw_ref, o_ref, in_scr, out_scr):
    i = pl.program_id(1)
    n = pl.num_programs(1)
    has_prev = (i > 0).astype(F32)
    has_next = (i < n - 1).astype(F32)
    rows = CONV_ROWS
    for cs in range(N_SLABS):
        sl = slice(cs * LANES, (cs + 1) * LANES)
        in_scr[cs, 0:HALO, :] = xp_ref[:, sl].astype(F32) * has_prev
        in_scr[cs, HALO:HALO + rows, :] = xm_ref[:, sl].astype(F32)
        in_scr[cs, HALO + rows:, :] = xn_ref[:, sl].astype(F32) * has_next

    def slab_body(cs, carry):
        cw = cw_ref[cs]
        taps = [jnp.broadcast_to(cw[w:w + 1, :], (8, LANES)) for w in range(CONV_W)]
        bias = jnp.broadcast_to(cw[CONV_W:CONV_W + 1, :], (8, LANES))
        for g in range(rows // 16):
            for e in range(2):
                acc = bias
                for w in range(CONV_W):
                    start = HALO + 16 * g + e + w - CONV_W // 2
                    acc = acc + in_scr[cs, pl.ds(start, 8, stride=2), :] * taps[w]
                out_scr[cs, pl.ds(16 * g + e, 8, stride=2), :] = _silu(acc)
        return carry

    lax.fori_loop(0, N_SLABS, slab_body, 0)
    for cs in range(N_SLABS):
        o_ref[:, cs * LANES:(cs + 1) * LANES] = out_scr[cs].astype(BF16)


def _conv_silu(proj3, conv_taps):
    batch, seq, _ = proj3.shape
    rows = CONV_ROWS
    hpr = rows // HALO
    nh = seq // HALO
    col = COL_XBC // CONV_DIM
    return pl.pallas_call(
        _conv_kernel,
        grid=(batch, seq // rows),
        in_specs=[pl.BlockSpec((None, HALO, CONV_DIM), lambda b, i: (b, jnp.maximum(i * hpr - 1, 0), col)),
                  pl.BlockSpec((None, rows, CONV_DIM), lambda b, i: (b, i, col)),
                  pl.BlockSpec((None, HALO, CONV_DIM), lambda b, i: (b, jnp.minimum((i + 1) * hpr, nh - 1), col)),
                  pl.BlockSpec((N_SLABS, 8, LANES), lambda b, i: (0, 0, 0))],
        out_specs=pl.BlockSpec((None, rows, CONV_DIM), lambda b, i: (b, i, 0)),
        out_shape=jax.ShapeDtypeStruct((batch, seq, CONV_DIM), BF16),
        scratch_shapes=[pltpu.VMEM((N_SLABS, rows + 2 * HALO, LANES), F32),
                        pltpu.VMEM((N_SLABS, rows, LANES), F32)],
        compiler_params=_cparams(("arbitrary", "arbitrary")),
        name="conv_silu",
    )(proj3, proj3, proj3, conv_taps)


def _ssd_direction(xa_ref, dt_ref, bias_ref, a_ref, e_ref, dskip_ref, st_ref, out_ref, *, reverse):
    lane0 = SSD_HEADS if reverse else 0
    xs = xa_ref[:, :D_INNER]
    bm = xa_ref[:, D_INNER:D_INNER + N_GROUPS * D_STATE]
    cm = xa_ref[:, D_INNER + N_GROUPS * D_STATE:]

    dt = jax.nn.softplus(dt_ref[...] + bias_ref[...])
    da = dt * a_ref[...]
    r = lax.broadcasted_iota(jnp.int32, (CHUNK, CHUNK), 0)
    c = lax.broadcasted_iota(jnp.int32, (CHUNK, CHUNK), 1)
    causal = (c >= r) if reverse else (c <= r)
    tri = causal.astype(BF16)
    d1 = da.astype(BF16)
    r1 = da - d1.astype(F32)
    d2 = r1.astype(BF16)
    d3 = (r1 - d2.astype(F32)).astype(BF16)
    cum = (jnp.dot(tri, d1, preferred_element_type=F32) + jnp.dot(tri, d2, preferred_element_type=F32)
           + jnp.dot(tri, d3, preferred_element_type=F32))
    src_t = (cum - jnp.log(dt)).T
    edge = CHUNK - 1 if not reverse else 0
    total = cum[edge:edge + 1, :]
    exp_cum = jnp.exp(cum)
    decay = jnp.exp(total - cum)

    e = e_ref[...]

    def expand(v):
        return jnp.dot(v.astype(BF16), e, preferred_element_type=F32)

    x_dec = xs * expand(dt * decay).astype(BF16)
    exp_cum_e = expand(exp_cum)
    tot8 = jnp.broadcast_to(jnp.exp(total), (8, DT_W))
    tot_hi = tot8.astype(BF16)
    tot_lo = (tot8 - tot_hi.astype(F32)).astype(BF16)
    chunk_decay = (jnp.dot(tot_hi, e, preferred_element_type=F32)
                   + jnp.dot(tot_lo, e, preferred_element_type=F32))[0:1, :]

    heads_per_group = SSD_HEADS // N_GROUPS
    gw = heads_per_group * SSD_HEAD_DIM
    lane = lax.broadcasted_iota(jnp.int32, (CHUNK, LANES), 1)
    neg = jnp.float32(-jnp.inf)

    def group_matmuls(g):
        b_g = bm[:, g * D_STATE:(g + 1) * D_STATE]
        c_g = cm[:, g * D_STATE:(g + 1) * D_STATE]
        cb = lax.dot_general(c_g, b_g, (((1,), (1,)), ((), ())), preferred_element_type=F32)
        gs = slice(g * gw, (g + 1) * gw)
        st_prev = st_ref[:, gs]
        y_off = jnp.dot(c_g, st_prev.astype(BF16), preferred_element_type=F32)
        upd = lax.dot_general(b_g, x_dec[:, gs], (((0,), (0,)), ((), ())), preferred_element_type=F32)
        st_ref[:, gs] = st_prev * chunk_decay[:, gs] + upd
        return cb, y_off

    def group_finish(g, mm):
        cb, y_off = mm
        gs = slice(g * gw, (g + 1) * gw)
        y = y_off * exp_cum_e[:, gs]
        y_diag = []
        for pair in range(heads_per_group // 2):
            ms = []
            for hh in range(2):
                k = lane0 + g * heads_per_group + pair * 2 + hh
                seg = cum[:, k:k + 1] - src_t[k:k + 1, :]
                ms.append((cb * jnp.exp(jnp.where(causal, seg, neg))).astype(BF16))
            col = g * gw + pair * LANES
            xp = xa_ref[:, col:col + LANES]
            zero = jnp.zeros_like(xp)
            rhs = jnp.concatenate([jnp.where(lane < SSD_HEAD_DIM, xp, zero),
                                   jnp.where(lane >= SSD_HEAD_DIM, xp, zero)], axis=0)
            y_diag.append(jnp.dot(jnp.concatenate(ms, axis=1), rhs, preferred_element_type=F32))
        y = y + jnp.concatenate(y_diag, axis=1)
        if dskip_ref is not None:
            y = y + dskip_ref[:, gs] * xs[:, gs].astype(F32)
        out_ref[:, gs] = y.astype(out_ref.dtype)

    return group_matmuls, group_finish


def _ssd_kernel(xa_f, dt_f, xa_b, dt_b, bias_ref, a_ref, e_f, e_b, dskip_ref, yf_ref, yb_ref, st_f, st_b):
    @pl.when(pl.program_id(1) == 0)
    def _():
        st_f[...] = jnp.zeros_like(st_f)
        st_b[...] = jnp.zeros_like(st_b)

    dirs = [_ssd_direction(xa_f, dt_f, bias_ref, a_ref, e_f, dskip_ref, st_f, yf_ref, reverse=False),
            _ssd_direction(xa_b, dt_b, bias_ref, a_ref, e_b, None, st_b, yb_ref, reverse=True)]
    pending = [mm(0) for mm, _ in dirs]
    for g in range(N_GROUPS):
        nxt = [mm(g + 1) for mm, _ in dirs] if g + 1 < N_GROUPS else None
        for (_, fin), mm_out in zip(dirs, pending):
            fin(g, mm_out)
        pending = nxt


def _ssd_scan(xact3, dt3, dt_bias, a_neg, e_f, e_b, dskip_e):
    batch, seq, _ = xact3.shape
    nc = seq // CHUNK

    def specs(chunk_of):
        return [pl.BlockSpec((None, CHUNK, CONV_DIM), lambda b, c: (b, chunk_of(c), 0)),
                pl.BlockSpec((None, CHUNK, DT_W), lambda b, c: (b, chunk_of(c), 0))]

    fwd = lambda c: c
    bwd = lambda c: nc - 1 - c
    const = lambda shape: pl.BlockSpec(shape, lambda b, c: (0,) * len(shape))
    out_f = pl.BlockSpec((None, CHUNK, D_INNER), lambda b, c: (b, c, 0))
    out_b = pl.BlockSpec((None, CHUNK, D_INNER), lambda b, c: (b, nc - 1 - c, 0))
    y_shape = jax.ShapeDtypeStruct((batch, seq, D_INNER), BF16)
    return pl.pallas_call(
        _ssd_kernel,
        grid=(batch, nc),
        in_specs=specs(fwd) + specs(bwd) + [
            const((1, DT_W)), const((1, DT_W)),
            const((DT_W, D_INNER)), const((DT_W, D_INNER)), const((1, D_INNER))],
        out_specs=[out_f, out_b],
        out_shape=[y_shape, y_shape],
        scratch_shapes=[pltpu.VMEM((D_STATE, D_INNER), F32), pltpu.VMEM((D_STATE, D_INNER), F32)],
        compiler_params=_cparams(("arbitrary", "arbitrary")),
        name="ssd_scan",
    )(xact3, dt3, xact3, dt3, dt_bias, a_neg, e_f, e_b, dskip_e)


def _outproj_kernel(x_ref, gt_ref, yf_ref, yb_ref, z_ref, attn_ref, ga_ref, gs_ref, nw_ref, w_ref, gpost_ref,
                    o_ref):
    y = (yf_ref[...].astype(F32) + yb_ref[...].astype(F32)) * _silu(z_ref[...].astype(F32))
    gw = D_INNER // N_GROUPS
    parts = []
    for g in range(N_GROUPS):
        yg = y[:, g * gw:(g + 1) * gw]
        parts.append(yg * lax.rsqrt(jnp.mean(yg * yg, axis=-1, keepdims=True) + EPS))
    ssd = jnp.concatenate(parts, axis=1) * nw_ref[...]
    merged = (jax.nn.sigmoid(ga_ref[...].astype(F32)) * attn_ref[...].astype(F32)
              + jax.nn.sigmoid(gs_ref[...].astype(F32)) * ssd)
    mix = jnp.dot(merged.astype(BF16), w_ref[...], preferred_element_type=F32)
    nrm = mix * lax.rsqrt(jnp.mean(mix * mix, axis=-1, keepdims=True) + EPS) * gpost_ref[...]
    o_ref[...] = x_ref[...] + gt_ref[...] * nrm


def _out_projection(x2, mod4, yf2, yb2, proj2, attn2, ssd_norm_w, w_out, g_post, batch, seq):
    t_rows = batch * seq
    tm = min(seq, 256)
    per_b = seq // tm
    row = lambda col: pl.BlockSpec((tm, D_MODEL), lambda i: (i, col))
    vec = pl.BlockSpec((1, D_MODEL), lambda i: (0, 0))
    return pl.pallas_call(
        _outproj_kernel,
        grid=(t_rows // tm,),
        in_specs=[row(0),
                  pl.BlockSpec((None, None, 1, D_MODEL), lambda i: (i // per_b, 2, 0, 0)),
                  row(0), row(0), row(COL_Z // D_MODEL), row(0),
                  row(COL_GATE // D_MODEL), row(COL_GATE // D_MODEL + 1),
                  vec,
                  pl.BlockSpec((D_MODEL, D_MODEL), lambda i: (0, 0)),
                  vec],
        out_specs=row(0),
        out_shape=jax.ShapeDtypeStruct((t_rows, D_MODEL), F32),
        compiler_params=_cparams(("arbitrary",)),
        name="merge_out_proj",
    )(x2, mod4, yf2, yb2, proj2, attn2, proj2, proj2, ssd_norm_w.reshape(1, D_INNER), w_out,
      g_post.reshape(1, D_MODEL))


def _ffn_kernel(x_ref, sc_ref, sh_ref, gt_ref, gpre_ref, wg_ref, wu_ref, wd_ref, gpost_ref, o_ref, h_scr, acc_scr):
    j = pl.program_id(1)

    @pl.when(j == 0)
    def _():
        _modulated_prenorm(x_ref, gpre_ref, sc_ref, sh_ref, h_scr)
        acc_scr[...] = jnp.zeros_like(acc_scr)

    h = h_scr[...]
    gate = jnp.dot(h, wg_ref[...], preferred_element_type=F32)
    up = jnp.dot(h, wu_ref[...], preferred_element_type=F32)
    act = (_silu(gate) * up).astype(BF16)
    acc_scr[...] += jnp.dot(act, wd_ref[...], preferred_element_type=F32)

    @pl.when(j == pl.num_programs(1) - 1)
    def _():
        _gated_postnorm_residual(acc_scr, x_ref, gpost_ref, gt_ref, o_ref)


def _ffn(x2, mod4, g_pre, w_g, w_u, w_d, g_post, batch, seq):
    t_rows = batch * seq
    tm = min(seq, 512)
    tf = 512
    per_b = seq // tm
    mod_spec = lambda k: pl.BlockSpec((None, None, 1, D_MODEL), lambda i, j: (i // per_b, k, 0, 0))
    vec = pl.BlockSpec((1, D_MODEL), lambda i, j: (0, 0))
    return pl.pallas_call(
        _ffn_kernel,
        grid=(t_rows // tm, D_FF // tf),
        in_specs=[pl.BlockSpec((tm, D_MODEL), lambda i, j: (i, 0)),
                  mod_spec(4), mod_spec(3), mod_spec(5), vec,
                  pl.BlockSpec((D_MODEL, tf), lambda i, j: (0, j)),
                  pl.BlockSpec((D_MODEL, tf), lambda i, j: (0, j)),
                  pl.BlockSpec((tf, D_MODEL), lambda i, j: (j, 0)),
                  vec],
        out_specs=pl.BlockSpec((tm, D_MODEL), lambda i, j: (i, 0)),
        out_shape=jax.ShapeDtypeStruct((t_rows, D_MODEL), F32),
        scratch_shapes=[pltpu.VMEM((tm, D_MODEL), BF16), pltpu.VMEM((tm, D_MODEL), F32)],
        compiler_params=_cparams(("arbitrary", "arbitrary")),
        name="swiglu_ffn",
    )(x2, mod4, mod4, mod4, g_pre.reshape(1, D_MODEL), w_g, w_u, w_d, g_post.reshape(1, D_MODEL))


def _prep_weights(w_in, conv_w, conv_b, a_log_f, a_log_b, dt_bias_f, dt_bias_b, d_skip, w_out, w_gu, w_down):
    o_q, o_k, o_v = 0, D_MODEL, D_MODEL + KV_WIDTH
    o_z = o_v + KV_WIDTH
    o_xbc = o_z + D_INNER
    o_dt = o_xbc + CONV_DIM
    o_gate = o_dt + 2 * SSD_HEADS
    seg = lambda off, width: w_in[:, off:off + width]
    w_main = jnp.concatenate([seg(o_xbc, CONV_DIM), seg(o_gate, 2 * D_MODEL), seg(o_q, D_MODEL),
                              seg(o_z, D_INNER), seg(o_k, KV_WIDTH), seg(o_v, KV_WIDTH)], axis=1).astype(BF16)
    w_dt = jnp.pad(seg(o_dt, 2 * SSD_HEADS), ((0, 0), (0, DT_W - 2 * SSD_HEADS))).astype(BF16)
    pad_l = lambda v: jnp.pad(v, (0, DT_W - v.shape[0])).reshape(1, DT_W)
    dt_bias = pad_l(jnp.concatenate([dt_bias_f, dt_bias_b]))
    a_neg = pad_l(-jnp.exp(jnp.concatenate([a_log_f, a_log_b])))
    taps8 = jnp.concatenate([conv_w, conv_b[None, :], jnp.zeros((8 - CONV_W - 1, CONV_DIM), F32)], axis=0)
    conv_taps = taps8.reshape(8, N_SLABS, LANES).transpose(1, 0, 2)
    head_of_lane = jnp.arange(D_INNER) // SSD_HEAD_DIM
    rows = jnp.arange(DT_W)[:, None]
    e_f = (rows == head_of_lane[None, :]).astype(BF16)
    e_b = (rows == head_of_lane[None, :] + SSD_HEADS).astype(BF16)
    dskip_e = jnp.repeat(d_skip, SSD_HEAD_DIM).reshape(1, D_INNER)
    return dict(w_main=w_main, w_dt=w_dt, dt_bias=dt_bias, a_neg=a_neg, conv_taps=conv_taps,
                e_f=e_f, e_b=e_b, dskip_e=dskip_e,
                w_out=w_out.astype(BF16), w_g=w_gu[:, :D_FF].astype(BF16), w_u=w_gu[:, D_FF:].astype(BF16),
                w_d=w_down.astype(BF16))


def _encoder_layer(x, mod4, tables, pw, g_pre1, g_post1, ssd_norm_w, sinks, g_pre2, g_post2):
    batch, seq, _ = x.shape
    x2 = x.reshape(batch * seq, D_MODEL)
    proj2, dt2 = _in_projection(x2, mod4, g_pre1, pw["w_main"], pw["w_dt"], tables, batch, seq)
    proj3 = proj2.reshape(batch, seq, PROJ_W)
    dt3 = dt2.reshape(batch, seq, DT_W)
    attn = _attention(proj3, sinks)
    xact3 = _conv_silu(proj3, pw["conv_taps"])
    yf, yb = _ssd_scan(xact3, dt3, pw["dt_bias"], pw["a_neg"], pw["e_f"], pw["e_b"], pw["dskip_e"])
    x1 = _out_projection(x2, mod4, yf.reshape(batch * seq, D_INNER), yb.reshape(batch * seq, D_INNER), proj2,
                         attn.reshape(batch * seq, D_MODEL), ssd_norm_w, pw["w_out"], g_post1, batch, seq)
    out = _ffn(x1, mod4, g_pre2, pw["w_g"], pw["w_u"], pw["w_d"], g_post2, batch, seq)
    return out.reshape(batch, seq, D_MODEL)


def kernel(x_prompt, x_sample, c_prompt, c_sample, w_ada, b_ada, g_pre1, g_post1, w_in, conv_w, conv_b, a_log_f,
           a_log_b, dt_bias_f, dt_bias_b, d_skip, ssd_norm_w, sinks, w_out, g_pre2, g_post2, w_gu, w_down):
    assert w_ada.shape[0] == 1, "single-layer problem"
    nb_p, nb_s = c_prompt.shape[0], c_sample.shape[0]
    nb = -(-(nb_p + nb_s) // 8) * 8
    c_all = jnp.pad(jnp.concatenate([c_prompt, c_sample], axis=0), ((0, nb - nb_p - nb_s), (0, 0)))
    mod = _modulation(c_all, w_ada[0], b_ada[0]).reshape(nb, 6, 1, D_MODEL)
    pw = _prep_weights(w_in[0], conv_w[0], conv_b[0], a_log_f[0], a_log_b[0], dt_bias_f[0], dt_bias_b[0],
                       d_skip[0], w_out[0], w_gu[0], w_down[0])
    outs = []
    for x, mod4 in ((x_prompt, mod[:nb_p]), (x_sample, mod[nb_p:nb_p + nb_s])):
        tables = _rope_tables(x.shape[1])
        outs.append(_encoder_layer(x, mod4, tables, pw, g_pre1[0], g_post1[0], ssd_norm_w[0], sinks[0],
                                   g_pre2[0], g_post2[0]))
    return tuple(outs)
```

```python
import functools
import math

import jax
import jax.numpy as jnp
from jax import lax
from jax.experimental import pallas as pl
from jax.experimental.pallas import tpu as pltpu

F32 = jnp.float32
BF16 = jnp.bfloat16

D_MODEL = 2048
N_HEADS = 32
N_KV_HEADS = 8
HEAD_DIM = 64
KV_WIDTH = N_KV_HEADS * HEAD_DIM
BLOCK = 128
ROPE_THETA = 500000.0
ROPE_DIM = HEAD_DIM // 4
SSD_HEADS = 32
SSD_HEAD_DIM = 64
D_INNER = SSD_HEADS * SSD_HEAD_DIM
D_STATE = 128
N_GROUPS = 8
CONV_W = 5
CONV_DIM = D_INNER + 2 * N_GROUPS * D_STATE
CHUNK = 128
D_FF = 5632
EPS = 1e-6
LOG2E = math.log2(math.e)

LANES = 128
HALO = 16

COL_XBC = 0
COL_GATE = COL_XBC + CONV_DIM
COL_Q = COL_GATE + 2 * D_MODEL
COL_Z = COL_Q + D_MODEL
COL_K = COL_Z + D_INNER
COL_V = COL_K + KV_WIDTH
PROJ_W = COL_V + KV_WIDTH
DT_W = LANES

VMEM_LIMIT = 56 * 1024 * 1024


def _cparams(sem):
    return pltpu.CompilerParams(dimension_semantics=sem, vmem_limit_bytes=VMEM_LIMIT)


def _silu(v):
    return v * jax.nn.sigmoid(v)


NORM_ROWS = 16


def _rowwise_rms_apply(src_ref, apply):
    n_rows, width = src_ref.shape
    v = src_ref[...]
    inv = lax.rsqrt(jnp.sum(v * v, axis=-1, keepdims=True) * (1.0 / width) + EPS)
    for r in range(n_rows // NORM_ROWS):
        rows = slice(r * NORM_ROWS, (r + 1) * NORM_ROWS)
        apply(rows, src_ref[rows, :], inv[rows])


def _modulated_prenorm(x_ref, g_ref, sc_ref, sh_ref, h_ref):
    tile = (NORM_ROWS, x_ref.shape[1])
    gmod = jnp.broadcast_to(g_ref[...] * (1.0 + sc_ref[...]), tile)
    shift = jnp.broadcast_to(sh_ref[...], tile)

    def apply(rows, x, inv):
        h_ref[rows, :] = (x * inv * gmod + shift).astype(BF16)

    _rowwise_rms_apply(x_ref, apply)


def _gated_postnorm_residual(f_ref, x_ref, gpost_ref, gt_ref, o_ref):
    gg = jnp.broadcast_to(gpost_ref[...] * gt_ref[...], (NORM_ROWS, f_ref.shape[1]))

    def apply(rows, f, inv):
        o_ref[rows, :] = x_ref[rows, :] + f * inv * gg

    _rowwise_rms_apply(f_ref, apply)


def _mod_kernel(c_ref, w_ref, b_ref, o_ref):
    c = c_ref[...]
    s = _silu(c).astype(BF16)
    o_ref[...] = jnp.dot(s, w_ref[...].astype(BF16), preferred_element_type=F32) + b_ref[...]


def _modulation(c_all, w_ada, b_ada):
    nb = c_all.shape[0]
    n = w_ada.shape[1]
    tn = 1024
    return pl.pallas_call(
        _mod_kernel,
        grid=(n // tn,),
        in_specs=[pl.BlockSpec((nb, D_MODEL), lambda j: (0, 0)),
                  pl.BlockSpec((D_MODEL, tn), lambda j: (0, j)),
                  pl.BlockSpec((1, tn), lambda j: (0, j))],
        out_specs=pl.BlockSpec((nb, tn), lambda j: (0, j)),
        out_shape=jax.ShapeDtypeStruct((nb, n), F32),
        compiler_params=_cparams(("arbitrary",)),
        name="adaln_mod",
    )(c_all, w_ada, b_ada.reshape(1, n))


def _rope_table_kernel(cos_ref, sin_lo_ref, sin_hi_ref):
    tm = cos_ref.shape[0]
    half = ROPE_DIM // 2
    pos = (pl.program_id(0) * tm + lax.broadcasted_iota(jnp.int32, (tm, LANES), 0)).astype(F32)
    d = lax.broadcasted_iota(jnp.int32, (tm, LANES), 1) % HEAD_DIM
    fidx = (d % half).astype(F32)
    inv = jnp.exp(fidx * (-2.0 / ROPE_DIM * math.log(ROPE_THETA)))
    ang = pos * inv
    cos = jnp.cos(ang)
    sin = jnp.sin(ang)
    cos_ref[...] = jnp.where(d < ROPE_DIM, cos, 1.0)
    sin_lo_ref[...] = jnp.where(d < half, -sin, 0.0)
    sin_hi_ref[...] = jnp.where((d >= half) & (d < ROPE_DIM), sin, 0.0)


def _rope_tables(seq):
    tm = min(seq, 2048)
    shp = jax.ShapeDtypeStruct((seq, LANES), F32)
    spec = pl.BlockSpec((tm, LANES), lambda i: (i, 0))
    return pl.pallas_call(
        _rope_table_kernel,
        grid=(seq // tm,),
        in_specs=[],
        out_specs=[spec, spec, spec],
        out_shape=[shp, shp, shp],
        compiler_params=_cparams(("arbitrary",)),
        name="rope_tables",
    )()


def _rope_slab(t, cos, sin_lo, sin_hi):
    half = ROPE_DIM // 2
    return t * cos + pltpu.roll(t, LANES - half, 1) * sin_lo + pltpu.roll(t, half, 1) * sin_hi


def _inproj_kernel(x_ref, sc_ref, sh_ref, g_ref, w_ref, wdt_ref, cos_ref, slo_ref, shi_ref,
                   proj_ref, dt_ref, h_scr, *, tn):
    j = pl.program_id(1)

    @pl.when(j == 0)
    def _():
        _modulated_prenorm(x_ref, g_ref, sc_ref, sh_ref, h_scr)
        dt_ref[...] = jnp.dot(h_scr[...], wdt_ref[...], preferred_element_type=F32)

    q_lo, q_hi = COL_Q // tn, (COL_Q + D_MODEL) // tn
    k_tile = COL_K // tn
    k_slabs = KV_WIDTH // LANES
    is_q = (j >= q_lo) & (j < q_hi)
    is_k = j == k_tile
    tm = h_scr.shape[0]
    n_parts = 2
    rows = tm // n_parts

    def project(n_rope_slabs, scale):
        for p in range(n_parts):
            rs = slice(p * rows, (p + 1) * rows)
            acc = jnp.dot(h_scr[rs, :], w_ref[...], preferred_element_type=F32)
            if n_rope_slabs == 0:
                proj_ref[rs, :] = acc.astype(BF16)
                continue
            cos, slo, shi = cos_ref[rs, :], slo_ref[rs, :], shi_ref[rs, :]
            for s in range(tn // LANES):
                t = acc[:, s * LANES:(s + 1) * LANES]
                if s < n_rope_slabs:
                    t = _rope_slab(t, cos, slo, shi)
                    if scale != 1.0:
                        t = t * scale
                proj_ref[rs, s * LANES:(s + 1) * LANES] = t.astype(BF16)

    @pl.when(is_q)
    def _():
        project(tn // LANES, HEAD_DIM ** -0.5 * LOG2E)

    @pl.when(is_k)
    def _():
        project(k_slabs, 1.0)

    @pl.when(jnp.logical_not(is_q | is_k))
    def _():
        project(0, 1.0)


def _in_projection(x2, mod4, g_pre, w_main, w_dt, tables, batch, seq):
    t_rows = batch * seq
    tm = min(seq, 1024)
    tn = 1024
    assert COL_K % tn == 0 and COL_Q % tn == 0 and D_MODEL % tn == 0 and PROJ_W % tn == 0
    per_b = seq // tm
    cos, slo, shi = tables
    tab_spec = pl.BlockSpec((tm, LANES), lambda i, j: (i % per_b, 0))
    mod_spec = lambda k: pl.BlockSpec((None, None, 1, D_MODEL), lambda i, j: (i // per_b, k, 0, 0))
    return pl.pallas_call(
        functools.partial(_inproj_kernel, tn=tn),
        grid=(t_rows // tm, PROJ_W // tn),
        in_specs=[pl.BlockSpec((tm, D_MODEL), lambda i, j: (i, 0)),
                  mod_spec(1), mod_spec(0),
                  pl.BlockSpec((1, D_MODEL), lambda i, j: (0, 0)),
                  pl.BlockSpec((D_MODEL, tn), lambda i, j: (0, j)),
                  pl.BlockSpec((D_MODEL, DT_W), lambda i, j: (0, 0)),
                  tab_spec, tab_spec, tab_spec],
        out_specs=[pl.BlockSpec((tm, tn), lambda i, j: (i, j)),
                   pl.BlockSpec((tm, DT_W), lambda i, j: (i, 0))],
        out_shape=[jax.ShapeDtypeStruct((t_rows, PROJ_W), BF16),
                   jax.ShapeDtypeStruct((t_rows, DT_W), F32)],
        scratch_shapes=[pltpu.VMEM((tm, D_MODEL), BF16)],
        compiler_params=_cparams(("arbitrary", "arbitrary")),
        name="norm_in_proj",
    )(x2, mod4, mod4, g_pre.reshape(1, D_MODEL), w_main, w_dt, cos, slo, shi)


def _attn_stages(sink_ref, q_ref, kp_ref, kc_ref, kn_ref, vp_ref, vc_ref, vn_ref, o_ref):
    i = pl.program_id(1)
    nblk = pl.num_programs(1)
    rows2 = 2 * BLOCK
    r = lax.broadcasted_iota(jnp.int32, (rows2, BLOCK), 0) % BLOCK
    c = lax.broadcasted_iota(jnp.int32, (rows2, BLOCK), 1)
    top = lax.broadcasted_iota(jnp.int32, (rows2, 1), 0) < BLOCK
    neg = jnp.float32(-jnp.inf)
    bias_p = jnp.where((c >= r) & (i > 0), 0.0, neg)
    bias_n = jnp.where((c <= r) & (i < nblk - 1), 0.0, neg)
    lane3 = lax.broadcasted_iota(jnp.int32, (3 * BLOCK, LANES), 1)
    low3 = lane3 < HEAD_DIM
    low2 = c < HEAD_DIM

    def padded_variants(p_ref, c_ref, n_ref, slab):
        sl = slice(slab * LANES, (slab + 1) * LANES)
        x = jnp.concatenate([p_ref[:, sl], c_ref[:, sl], n_ref[:, sl]], axis=0)
        xr = pltpu.bitcast(pltpu.roll(pltpu.bitcast(x, jnp.uint32), HEAD_DIM, 1), BF16)
        zero = jnp.zeros_like(x)
        return ((jnp.where(low3, x, zero), jnp.where(low3, zero, xr)),
                (jnp.where(low3, xr, zero), jnp.where(low3, zero, x)))

    variants = {}

    def scores(g):
        slab, par = divmod(g, 2)
        if slab not in variants:
            variants[slab] = (padded_variants(kp_ref, kc_ref, kn_ref, slab),
                              padded_variants(vp_ref, vc_ref, vn_ref, slab))
        k_var, v_var = variants[slab]
        qq = jnp.concatenate([q_ref[:, (2 * g) * LANES:(2 * g + 1) * LANES],
                              q_ref[:, (2 * g + 1) * LANES:(2 * g + 2) * LANES]], axis=0)
        kk = jnp.concatenate(k_var[par], axis=0)
        s = lax.dot_general(qq, kk, (((1,), (1,)), ((), ())), preferred_element_type=F32)
        return s, jnp.concatenate(v_var[par], axis=0)

    def finish(g, s, vv):
        probs, inv = [], []
        for e in range(2):
            base = e * 3 * BLOCK
            sp = s[:, base:base + BLOCK] + bias_p
            sc = s[:, base + BLOCK:base + 2 * BLOCK]
            sn = s[:, base + 2 * BLOCK:base + 3 * BLOCK] + bias_n
            sink = jnp.where(top, sink_ref[4 * g + e], sink_ref[4 * g + 2 + e]) * LOG2E
            m = jnp.maximum(jnp.max(jnp.maximum(jnp.maximum(sp, sc), sn), axis=-1, keepdims=True), sink)
            pp, pc, pn = jnp.exp2(sp - m), jnp.exp2(sc - m), jnp.exp2(sn - m)
            denom = jnp.sum(pp + pc + pn, axis=-1, keepdims=True) + jnp.exp2(sink - m)
            probs += [pp.astype(BF16), pc.astype(BF16), pn.astype(BF16)]
            inv.append(1.0 / denom)
        o = jnp.dot(jnp.concatenate(probs, axis=1), vv, preferred_element_type=F32)
        o = o * jnp.where(low2, inv[0], inv[1])
        o_ref[:, (2 * g) * LANES:(2 * g + 1) * LANES] = o[:BLOCK].astype(BF16)
        o_ref[:, (2 * g + 1) * LANES:(2 * g + 2) * LANES] = o[BLOCK:].astype(BF16)

    return scores, finish


CONV_ROWS = 256
N_SLABS = CONV_DIM // LANES


def _conv_kernel(xp_ref, xm_ref, xn_ref, cw_ref, o_ref, in_scr, out_scr):
    i = pl.program_id(1)
    n = pl.num_programs(1)
    has_prev = (i > 0).astype(F32)
    has_next = (i < n - 1).astype(F32)
    rows = CONV_ROWS
    for cs in range(N_SLABS):
        sl = slice(cs * LANES, (cs + 1) * LANES)
        in_scr[cs, 0:HALO, :] = xp_ref[:, sl].astype(F32) * has_prev
        in_scr[cs, HALO:HALO + rows, :] = xm_ref[:, sl].astype(F32)
        in_scr[cs, HALO + rows:, :] = xn_ref[:, sl].astype(F32) * has_next

    def slab_body(cs, carry):
        cw = cw_ref[cs]
        taps = [jnp.broadcast_to(cw[w:w + 1, :], (8, LANES)) for w in range(CONV_W)]
        bias = jnp.broadcast_to(cw[CONV_W:CONV_W + 1, :], (8, LANES))
        for g in range(rows // 16):
            for e in range(2):
                acc = bias
                for w in range(CONV_W):
                    start = HALO + 16 * g + e + w - CONV_W // 2
                    acc = acc + in_scr[cs, pl.ds(start, 8, stride=2), :] * taps[w]
                out_scr[cs, pl.ds(16 * g + e, 8, stride=2), :] = _silu(acc)
        return carry

    lax.fori_loop(0, N_SLABS, slab_body, 0)
    for cs in range(N_SLABS):
        o_ref[:, cs * LANES:(cs + 1) * LANES] = out_scr[cs].astype(BF16)


def _conv_silu(proj3, conv_taps):
    batch, seq, _ = proj3.shape
    rows = CONV_ROWS
    hpr = rows // HALO
    nh = seq // HALO
    col = COL_XBC // CONV_DIM
    return pl.pallas_call(
        _conv_kernel,
        grid=(batch, seq // rows),
        in_specs=[pl.BlockSpec((None, HALO, CONV_DIM), lambda b, i: (b, jnp.maximum(i * hpr - 1, 0), col)),
                  pl.BlockSpec((None, rows, CONV_DIM), lambda b, i: (b, i, col)),
                  pl.BlockSpec((None, HALO, CONV_DIM), lambda b, i: (b, jnp.minimum((i + 1) * hpr, nh - 1), col)),
                  pl.BlockSpec((N_SLABS, 8, LANES), lambda b, i: (0, 0, 0))],
        out_specs=pl.BlockSpec((None, rows, CONV_DIM), lambda b, i: (b, i, 0)),
        out_shape=jax.ShapeDtypeStruct((batch, seq, CONV_DIM), BF16),
        scratch_shapes=[pltpu.VMEM((N_SLABS, rows + 2 * HALO, LANES), F32),
                        pltpu.VMEM((N_SLABS, rows, LANES), F32)],
        compiler_params=_cparams(("arbitrary", "arbitrary")),
        name="conv_silu",
    )(proj3, proj3, proj3, conv_taps)


def _ssd_direction(xa_ref, dt_ref, bias_ref, a_ref, e_ref, dskip_ref, st_ref, out_ref, *, reverse):
    lane0 = SSD_HEADS if reverse else 0
    xs = xa_ref[:, :D_INNER]
    bm = xa_ref[:, D_INNER:D_INNER + N_GROUPS * D_STATE]
    cm = xa_ref[:, D_INNER + N_GROUPS * D_STATE:]

    dt = jax.nn.softplus(dt_ref[...] + bias_ref[...])
    da = dt * a_ref[...]
    r = lax.broadcasted_iota(jnp.int32, (CHUNK, CHUNK), 0)
    c = lax.broadcasted_iota(jnp.int32, (CHUNK, CHUNK), 1)
    causal = (c >= r) if reverse else (c <= r)
    tri = causal.astype(BF16)
    d1 = da.astype(BF16)
    r1 = da - d1.astype(F32)
    d2 = r1.astype(BF16)
    d3 = (r1 - d2.astype(F32)).astype(BF16)
    cum = (jnp.dot(tri, d1, preferred_element_type=F32) + jnp.dot(tri, d2, preferred_element_type=F32)
           + jnp.dot(tri, d3, preferred_element_type=F32))
    src_t = (cum - jnp.log(dt)).T
    edge = CHUNK - 1 if not reverse else 0
    total = cum[edge:edge + 1, :]
    exp_cum = jnp.exp(cum)
    decay = jnp.exp(total - cum)

    e = e_ref[...]

    def expand(v):
        return jnp.dot(v.astype(BF16), e, preferred_element_type=F32)

    x_dec = xs * expand(dt * decay).astype(BF16)
    exp_cum_e = expand(exp_cum)
    tot8 = jnp.broadcast_to(jnp.exp(total), (8, DT_W))
    tot_hi = tot8.astype(BF16)
    tot_lo = (tot8 - tot_hi.astype(F32)).astype(BF16)
    chunk_decay = (jnp.dot(tot_hi, e, preferred_element_type=F32)
                   + jnp.dot(tot_lo, e, preferred_element_type=F32))[0:1, :]

    heads_per_group = SSD_HEADS // N_GROUPS
    gw = heads_per_group * SSD_HEAD_DIM
    lane = lax.broadcasted_iota(jnp.int32, (CHUNK, LANES), 1)
    neg = jnp.float32(-jnp.inf)

    def group_matmuls(g):
        b_g = bm[:, g * D_STATE:(g + 1) * D_STATE]
        c_g = cm[:, g * D_STATE:(g + 1) * D_STATE]
        cb = lax.dot_general(c_g, b_g, (((1,), (1,)), ((), ())), preferred_element_type=F32)
        gs = slice(g * gw, (g + 1) * gw)
        st_prev = st_ref[:, gs]
        y_off = jnp.dot(c_g, st_prev.astype(BF16), preferred_element_type=F32)
        upd = lax.dot_general(b_g, x_dec[:, gs], (((0,), (0,)), ((), ())), preferred_element_type=F32)
        st_ref[:, gs] = st_prev * chunk_decay[:, gs] + upd
        return cb, y_off

    def group_finish(g, mm):
        cb, y_off = mm
        gs = slice(g * gw, (g + 1) * gw)
        y = y_off * exp_cum_e[:, gs]
        y_diag = []
        for pair in range(heads_per_group // 2):
            ms = []
            for hh in range(2):
                k = lane0 + g * heads_per_group + pair * 2 + hh
                seg = cum[:, k:k + 1] - src_t[k:k + 1, :]
                ms.append((cb * jnp.exp(jnp.where(causal, seg, neg))).astype(BF16))
            col = g * gw + pair * LANES
            xp = xa_ref[:, col:col + LANES]
            zero = jnp.zeros_like(xp)
            rhs = jnp.concatenate([jnp.where(lane < SSD_HEAD_DIM, xp, zero),
                                   jnp.where(lane >= SSD_HEAD_DIM, xp, zero)], axis=0)
            y_diag.append(jnp.dot(jnp.concatenate(ms, axis=1), rhs, preferred_element_type=F32))
        y = y + jnp.concatenate(y_diag, axis=1)
        if dskip_ref is not None:
            y = y + dskip_ref[:, gs] * xs[:, gs].astype(F32)
        out_ref[:, gs] = y.astype(out_ref.dtype)

    return group_matmuls, group_finish


def _mixers_kernel(sink_ref, q_ref, kp_ref, kc_ref, kn_ref, vp_ref, vc_ref, vn_ref,
                   xa_f, dt_f, xa_b, dt_b, bias_ref, a_ref, e_f, e_b, dskip_ref,
                   attn_ref, yf_ref, yb_ref, st_f, st_b):
    @pl.when(pl.program_id(1) == 0)
    def _():
        st_f[...] = jnp.zeros_like(st_f)
        st_b[...] = jnp.zeros_like(st_b)

    scores, attn_finish = _attn_stages(sink_ref, q_ref, kp_ref, kc_ref, kn_ref, vp_ref, vc_ref, vn_ref, attn_ref)
    dirs = [_ssd_direction(xa_f, dt_f, bias_ref, a_ref, e_f, dskip_ref, st_f, yf_ref, reverse=False),
            _ssd_direction(xa_b, dt_b, bias_ref, a_ref, e_b, None, st_b, yb_ref, reverse=True)]
    assert N_GROUPS == N_KV_HEADS
    pend_a = scores(0)
    pend_s = [mm(0) for mm, _ in dirs]
    for g in range(N_GROUPS):
        last = g + 1 == N_GROUPS
        next_a = None if last else scores(g + 1)
        next_s = None if last else [mm(g + 1) for mm, _ in dirs]
        attn_finish(g, *pend_a)
        for (_, fin), mm_out in zip(dirs, pend_s):
            fin(g, mm_out)
        pend_a, pend_s = next_a, next_s


def _token_mixers(proj3, xact3, dt3, sinks, dt_bias, a_neg, e_f, e_b, dskip_e):
    batch, seq, _ = proj3.shape
    n = seq // BLOCK
    assert BLOCK == CHUNK
    kcol, vcol = COL_K // KV_WIDTH, COL_V // KV_WIDTH
    prev = lambda col: pl.BlockSpec((None, BLOCK, KV_WIDTH), lambda b, i: (b, jnp.maximum(i - 1, 0), col))
    cur = lambda col: pl.BlockSpec((None, BLOCK, KV_WIDTH), lambda b, i: (b, i, col))
    nxt = lambda col: pl.BlockSpec((None, BLOCK, KV_WIDTH), lambda b, i: (b, jnp.minimum(i + 1, n - 1), col))

    def ssd_specs(chunk_of):
        return [pl.BlockSpec((None, CHUNK, CONV_DIM), lambda b, i: (b, chunk_of(i), 0)),
                pl.BlockSpec((None, CHUNK, DT_W), lambda b, i: (b, chunk_of(i), 0))]

    fwd = lambda i: i
    bwd = lambda i: n - 1 - i
    const = lambda shape: pl.BlockSpec(shape, lambda b, i: (0,) * len(shape))
    row_f = pl.BlockSpec((None, BLOCK, D_MODEL), lambda b, i: (b, i, 0))
    row_b = pl.BlockSpec((None, BLOCK, D_MODEL), lambda b, i: (b, n - 1 - i, 0))
    shape = jax.ShapeDtypeStruct((batch, seq, D_MODEL), BF16)
    return pl.pallas_call(
        _mixers_kernel,
        grid=(batch, n),
        in_specs=[pl.BlockSpec(memory_space=pltpu.SMEM),
                  pl.BlockSpec((None, BLOCK, D_MODEL), lambda b, i: (b, i, COL_Q // D_MODEL)),
                  prev(kcol), cur(kcol), nxt(kcol), prev(vcol), cur(vcol), nxt(vcol)]
                 + ssd_specs(fwd) + ssd_specs(bwd)
                 + [const((1, DT_W)), const((1, DT_W)), const((DT_W, D_INNER)), const((DT_W, D_INNER)),
                    const((1, D_INNER))],
        out_specs=[row_f, row_f, row_b],
        out_shape=[shape, shape, shape],
        scratch_shapes=[pltpu.VMEM((D_STATE, D_INNER), F32), pltpu.VMEM((D_STATE, D_INNER), F32)],
        compiler_params=_cparams(("arbitrary", "arbitrary")),
        name="token_mixers",
    )(sinks, proj3, proj3, proj3, proj3, proj3, proj3, proj3, xact3, dt3, xact3, dt3,
      dt_bias, a_neg, e_f, e_b, dskip_e)


def _outproj_kernel(x_ref, gt_ref, yf_ref, yb_ref, z_ref, attn_ref, ga_ref, gs_ref, nw_ref, w_ref, gpost_ref,
                    o_ref):
    z = z_ref[...]
    y = (yf_ref[...].astype(F32) + yb_ref[...].astype(F32)) * (z * jax.nn.sigmoid(z)).astype(F32)
    gw = D_INNER // N_GROUPS
    parts = []
    for g in range(N_GROUPS):
        yg = y[:, g * gw:(g + 1) * gw]
        parts.append(yg * lax.rsqrt(jnp.mean(yg * yg, axis=-1, keepdims=True) + EPS))
    ssd = (jnp.concatenate(parts, axis=1) * nw_ref[...]).astype(BF16)
    merged = jax.nn.sigmoid(ga_ref[...]) * attn_ref[...] + jax.nn.sigmoid(gs_ref[...]) * ssd
    mix = jnp.dot(merged, w_ref[...], preferred_element_type=F32)
    nrm = mix * lax.rsqrt(jnp.mean(mix * mix, axis=-1, keepdims=True) + EPS) * gpost_ref[...]
    o_ref[...] = x_ref[...] + gt_ref[...] * nrm


def _out_projection(x2, mod4, yf2, yb2, proj2, attn2, ssd_norm_w, w_out, g_post, batch, seq):
    t_rows = batch * seq
    tm = min(seq, 256)
    per_b = seq // tm
    row = lambda col: pl.BlockSpec((tm, D_MODEL), lambda i: (i, col))
    vec = pl.BlockSpec((1, D_MODEL), lambda i: (0, 0))
    return pl.pallas_call(
        _outproj_kernel,
        grid=(t_rows // tm,),
        in_specs=[row(0),
                  pl.BlockSpec((None, None, 1, D_MODEL), lambda i: (i // per_b, 2, 0, 0)),
                  row(0), row(0), row(COL_Z // D_MODEL), row(0),
                  row(COL_GATE // D_MODEL), row(COL_GATE // D_MODEL + 1),
                  vec,
                  pl.BlockSpec((D_MODEL, D_MODEL), lambda i: (0, 0)),
                  vec],
        out_specs=row(0),
        out_shape=jax.ShapeDtypeStruct((t_rows, D_MODEL), F32),
        compiler_params=_cparams(("arbitrary",)),
        name="merge_out_proj",
    )(x2, mod4, yf2, yb2, proj2, attn2, proj2, proj2, ssd_norm_w.reshape(1, D_INNER), w_out,
      g_post.reshape(1, D_MODEL))


def _ffn_kernel(x_ref, sc_ref, sh_ref, gt_ref, gpre_ref, wg_ref, wu_ref, wd_ref, gpost_ref, o_ref, h_scr, acc_scr):
    j = pl.program_id(1)

    @pl.when(j == 0)
    def _():
        _modulated_prenorm(x_ref, gpre_ref, sc_ref, sh_ref, h_scr)
        acc_scr[...] = jnp.zeros_like(acc_scr)

    h = h_scr[...]
    gate = jnp.dot(h, wg_ref[...], preferred_element_type=F32)
    up = jnp.dot(h, wu_ref[...], preferred_element_type=F32)
    act = (_silu(gate) * up).astype(BF16)
    acc_scr[...] += jnp.dot(act, wd_ref[...], preferred_element_type=F32)

    @pl.when(j == pl.num_programs(1) - 1)
    def _():
        _gated_postnorm_residual(acc_scr, x_ref, gpost_ref, gt_ref, o_ref)


def _ffn(x2, mod4, g_pre, w_gu, w_d, g_post, batch, seq):
    t_rows = batch * seq
    tm = min(seq, 512)
    tf = 512
    up0 = D_FF // tf
    per_b = seq // tm
    mod_spec = lambda k: pl.BlockSpec((None, None, 1, D_MODEL), lambda i, j: (i // per_b, k, 0, 0))
    vec = pl.BlockSpec((1, D_MODEL), lambda i, j: (0, 0))
    return pl.pallas_call(
        _ffn_kernel,
        grid=(t_rows // tm, D_FF // tf),
        in_specs=[pl.BlockSpec((tm, D_MODEL), lambda i, j: (i, 0)),
                  mod_spec(4), mod_spec(3), mod_spec(5), vec,
                  pl.BlockSpec((D_MODEL, tf), lambda i, j: (0, j)),
                  pl.BlockSpec((D_MODEL, tf), lambda i, j: (0, up0 + j)),
                  pl.BlockSpec((tf, D_MODEL), lambda i, j: (j, 0)),
                  vec],
        out_specs=pl.BlockSpec((tm, D_MODEL), lambda i, j: (i, 0)),
        out_shape=jax.ShapeDtypeStruct((t_rows, D_MODEL), F32),
        scratch_shapes=[pltpu.VMEM((tm, D_MODEL), BF16), pltpu.VMEM((tm, D_MODEL), F32)],
        compiler_params=_cparams(("arbitrary", "arbitrary")),
        name="swiglu_ffn",
    )(x2, mod4, mod4, mod4, g_pre.reshape(1, D_MODEL), w_gu, w_gu, w_d, g_post.reshape(1, D_MODEL))


def _prep_weights(w_in, conv_w, conv_b, a_log_f, a_log_b, dt_bias_f, dt_bias_b, d_skip, w_out, w_gu, w_down):
    o_q, o_k, o_v = 0, D_MODEL, D_MODEL + KV_WIDTH
    o_z = o_v + KV_WIDTH
    o_xbc = o_z + D_INNER
    o_dt = o_xbc + CONV_DIM
    o_gate = o_dt + 2 * SSD_HEADS
    seg = lambda off, width: w_in[:, off:off + width]
    w_main = jnp.concatenate([seg(o_xbc, CONV_DIM), seg(o_gate, 2 * D_MODEL), seg(o_q, D_MODEL),
                              seg(o_z, D_INNER), seg(o_k, KV_WIDTH), seg(o_v, KV_WIDTH)], axis=1).astype(BF16)
    w_dt = jnp.pad(seg(o_dt, 2 * SSD_HEADS), ((0, 0), (0, DT_W - 2 * SSD_HEADS))).astype(BF16)
    pad_l = lambda v: jnp.pad(v, (0, DT_W - v.shape[0])).reshape(1, DT_W)
    dt_bias = pad_l(jnp.concatenate([dt_bias_f, dt_bias_b]))
    a_neg = pad_l(-jnp.exp(jnp.concatenate([a_log_f, a_log_b])))
    taps8 = jnp.concatenate([conv_w, conv_b[None, :], jnp.zeros((8 - CONV_W - 1, CONV_DIM), F32)], axis=0)
    conv_taps = taps8.reshape(8, N_SLABS, LANES).transpose(1, 0, 2)
    head_of_lane = jnp.arange(D_INNER) // SSD_HEAD_DIM
    rows = jnp.arange(DT_W)[:, None]
    e_f = (rows == head_of_lane[None, :]).astype(BF16)
    e_b = (rows == head_of_lane[None, :] + SSD_HEADS).astype(BF16)
    dskip_e = jnp.repeat(d_skip, SSD_HEAD_DIM).reshape(1, D_INNER)
    return dict(w_main=w_main, w_dt=w_dt, dt_bias=dt_bias, a_neg=a_neg, conv_taps=conv_taps,
                e_f=e_f, e_b=e_b, dskip_e=dskip_e,
                w_out=w_out.astype(BF16), w_gu=w_gu.astype(BF16),
                w_d=w_down.astype(BF16))


def _encoder_layer(x, mod4, tables, pw, g_pre1, g_post1, ssd_norm_w, sinks, g_pre2, g_post2):
    batch, seq, _ = x.shape
    x2 = x.reshape(batch * seq, D_MODEL)
    proj2, dt2 = _in_projection(x2, mod4, g_pre1, pw["w_main"], pw["w_dt"], tables, batch, seq)
    proj3 = proj2.reshape(batch, seq, PROJ_W)
    dt3 = dt2.reshape(batch, seq, DT_W)
    xact3 = _conv_silu(proj3, pw["conv_taps"])
    attn, yf, yb = _token_mixers(proj3, xact3, dt3, sinks, pw["dt_bias"], pw["a_neg"], pw["e_f"], pw["e_b"],
                                 pw["dskip_e"])
    x1 = _out_projection(x2, mod4, yf.reshape(batch * seq, D_INNER), yb.reshape(batch * seq, D_INNER), proj2,
                         attn.reshape(batch * seq, D_MODEL), ssd_norm_w, pw["w_out"], g_post1, batch, seq)
    out = _ffn(x1, mod4, g_pre2, pw["w_gu"], pw["w_d"], g_post2, batch, seq)
    return out.reshape(batch, seq, D_MODEL)


def kernel(x_prompt, x_sample, c_prompt, c_sample, w_ada, b_ada, g_pre1, g_post1, w_in, conv_w, conv_b, a_log_f,
           a_log_b, dt_bias_f, dt_bias_b, d_skip, ssd_norm_w, sinks, w_out, g_pre2, g_post2, w_gu, w_down):
    assert w_ada.shape[0] == 1, "single-layer problem"
    nb_p, nb_s = c_prompt.shape[0], c_sample.shape[0]
    nb = -(-(nb_p + nb_s) // 8) * 8
    c_all = jnp.pad(jnp.concatenate([c_prompt, c_sample], axis=0), ((0, nb - nb_p - nb_s), (0, 0)))
    mod = _modulation(c_all, w_ada[0], b_ada[0]).reshape(nb, 6, 1, D_MODEL)
    pw = _prep_weights(w_in[0], conv_w[0], conv_b[0], a_log_f[0], a_log_b[0], dt_bias_f[0], dt_bias_b[0],
                       d_skip[0], w_out[0], w_gu[0], w_down[0])
    outs = []
    for x, mod4 in ((x_prompt, mod[:nb_p]), (x_sample, mod[nb_p:nb_p + nb_s])):
        tables = _rope_tables(x.shape[1])
        outs.append(_encoder_layer(x, mod4, tables, pw, g_pre1[0], g_post1[0], ssd_norm_w[0], sinks[0],
                                   g_pre2[0], g_post2[0]))
    return tuple(outs)
```

```python
import functools
import math

import jax
import jax.numpy as jnp
from jax import lax
from jax.experimental import pallas as pl
from jax.experimental.pallas import tpu as pltpu

F32 = jnp.float32
BF16 = jnp.bfloat16

D_MODEL = 2048
N_HEADS = 32
N_KV_HEADS = 8
HEAD_DIM = 64
KV_WIDTH = N_KV_HEADS * HEAD_DIM
BLOCK = 128
ROPE_THETA = 500000.0
ROPE_DIM = HEAD_DIM // 4
SSD_HEADS = 32
SSD_HEAD_DIM = 64
D_INNER = SSD_HEADS * SSD_HEAD_DIM
D_STATE = 128
N_GROUPS = 8
CONV_W = 5
CONV_DIM = D_INNER + 2 * N_GROUPS * D_STATE
CHUNK = 128
D_FF = 5632
EPS = 1e-6
LOG2E = math.log2(math.e)

LANES = 128
HALO = 16

COL_XBC = 0
COL_GATE = COL_XBC + CONV_DIM
COL_Q = COL_GATE + 2 * D_MODEL
COL_Z = COL_Q + D_MODEL
COL_K = COL_Z + D_INNER
COL_V = COL_K + KV_WIDTH
PROJ_W = COL_V + KV_WIDTH
DT_W = LANES

VMEM_LIMIT = 56 * 1024 * 1024


def _cparams(sem):
    return pltpu.CompilerParams(dimension_semantics=sem, vmem_limit_bytes=VMEM_LIMIT)


def _silu(v):
    return v * jax.nn.sigmoid(v)


NORM_ROWS = 16


def _rowwise_rms_apply(src_ref, apply):
    n_rows, width = src_ref.shape
    v = src_ref[...]
    inv = lax.rsqrt(jnp.sum(v * v, axis=-1, keepdims=True) * (1.0 / width) + EPS)
    for r in range(n_rows // NORM_ROWS):
        rows = slice(r * NORM_ROWS, (r + 1) * NORM_ROWS)
        apply(rows, src_ref[rows, :], inv[rows])


def _modulated_prenorm(x_ref, g_ref, sc_ref, sh_ref, h_ref):
    tile = (NORM_ROWS, x_ref.shape[1])
    gmod = jnp.broadcast_to(g_ref[...] * (1.0 + sc_ref[...]), tile)
    shift = jnp.broadcast_to(sh_ref[...], tile)

    def apply(rows, x, inv):
        h_ref[rows, :] = (x * inv * gmod + shift).astype(BF16)

    _rowwise_rms_apply(x_ref, apply)


def _gated_postnorm_residual(f_ref, x_ref, gpost_ref, gt_ref, o_ref):
    gg = jnp.broadcast_to(gpost_ref[...] * gt_ref[...], (NORM_ROWS, f_ref.shape[1]))

    def apply(rows, f, inv):
        o_ref[rows, :] = x_ref[rows, :] + f * inv * gg

    _rowwise_rms_apply(f_ref, apply)


def _mod_kernel(c_ref, w_ref, b_ref, o_ref):
    c = c_ref[...]
    s = _silu(c).astype(BF16)
    o_ref[...] = jnp.dot(s, w_ref[...].astype(BF16), preferred_element_type=F32) + b_ref[...]


def _modulation(c_all, w_ada, b_ada):
    nb = c_all.shape[0]
    n = w_ada.shape[1]
    tn = 1024
    return pl.pallas_call(
        _mod_kernel,
        grid=(n // tn,),
        in_specs=[pl.BlockSpec((nb, D_MODEL), lambda j: (0, 0)),
                  pl.BlockSpec((D_MODEL, tn), lambda j: (0, j)),
                  pl.BlockSpec((1, tn), lambda j: (0, j))],
        out_specs=pl.BlockSpec((nb, tn), lambda j: (0, j)),
        out_shape=jax.ShapeDtypeStruct((nb, n), F32),
        compiler_params=_cparams(("arbitrary",)),
        name="adaln_mod",
    )(c_all, w_ada, b_ada.reshape(1, n))


def _rope_table_kernel(cos_ref, sin_lo_ref, sin_hi_ref):
    tm = cos_ref.shape[0]
    half = ROPE_DIM // 2
    pos = (pl.program_id(0) * tm + lax.broadcasted_iota(jnp.int32, (tm, LANES), 0)).astype(F32)
    d = lax.broadcasted_iota(jnp.int32, (tm, LANES), 1) % HEAD_DIM
    fidx = (d % half).astype(F32)
    inv = jnp.exp(fidx * (-2.0 / ROPE_DIM * math.log(ROPE_THETA)))
    ang = pos * inv
    cos = jnp.cos(ang)
    sin = jnp.sin(ang)
    cos_ref[...] = jnp.where(d < ROPE_DIM, cos, 1.0)
    sin_lo_ref[...] = jnp.where(d < half, -sin, 0.0)
    sin_hi_ref[...] = jnp.where((d >= half) & (d < ROPE_DIM), sin, 0.0)


def _rope_tables(seq):
    tm = min(seq, 2048)
    shp = jax.ShapeDtypeStruct((seq, LANES), F32)
    spec = pl.BlockSpec((tm, LANES), lambda i: (i, 0))
    return pl.pallas_call(
        _rope_table_kernel,
        grid=(seq // tm,),
        in_specs=[],
        out_specs=[spec, spec, spec],
        out_shape=[shp, shp, shp],
        compiler_params=_cparams(("arbitrary",)),
        name="rope_tables",
    )()


def _rope_slab(t, cos, sin_lo, sin_hi):
    half = ROPE_DIM // 2
    return t * cos + pltpu.roll(t, LANES - half, 1) * sin_lo + pltpu.roll(t, half, 1) * sin_hi


def _inproj_kernel(x_ref, sc_ref, sh_ref, g_ref, w_ref, wdt_ref, cos_ref, slo_ref, shi_ref,
                   proj_ref, dt_ref, h_scr, *, tn):
    j = pl.program_id(1)

    @pl.when(j == 0)
    def _():
        _modulated_prenorm(x_ref, g_ref, sc_ref, sh_ref, h_scr)
        dt_ref[...] = jnp.dot(h_scr[...], wdt_ref[...], preferred_element_type=F32)

    q_lo, q_hi = COL_Q // tn, (COL_Q + D_MODEL) // tn
    k_tile = COL_K // tn
    k_slabs = KV_WIDTH // LANES
    is_q = (j >= q_lo) & (j < q_hi)
    is_k = j == k_tile
    tm = h_scr.shape[0]
    n_parts = 2
    rows = tm // n_parts

    def project(n_rope_slabs, scale):
        for p in range(n_parts):
            rs = slice(p * rows, (p + 1) * rows)
            acc = jnp.dot(h_scr[rs, :], w_ref[...], preferred_element_type=F32)
            if n_rope_slabs == 0:
                proj_ref[rs, :] = acc.astype(BF16)
                continue
            cos, slo, shi = cos_ref[rs, :], slo_ref[rs, :], shi_ref[rs, :]
            for s in range(tn // LANES):
                t = acc[:, s * LANES:(s + 1) * LANES]
                if s < n_rope_slabs:
                    t = _rope_slab(t, cos, slo, shi)
                    if scale != 1.0:
                        t = t * scale
                proj_ref[rs, s * LANES:(s + 1) * LANES] = t.astype(BF16)

    @pl.when(is_q)
    def _():
        project(tn // LANES, HEAD_DIM ** -0.5 * LOG2E)

    @pl.when(is_k)
    def _():
        project(k_slabs, 1.0)

    @pl.when(jnp.logical_not(is_q | is_k))
    def _():
        project(0, 1.0)


def _in_projection(x2, mod4, g_pre, w_main, w_dt, tables, batch, seq):
    t_rows = batch * seq
    tm = min(seq, 1024)
    tn = 1024
    assert COL_K % tn == 0 and COL_Q % tn == 0 and D_MODEL % tn == 0 and PROJ_W % tn == 0
    per_b = seq // tm
    cos, slo, shi = tables
    tab_spec = pl.BlockSpec((tm, LANES), lambda i, j: (i % per_b, 0))
    mod_spec = lambda k: pl.BlockSpec((None, None, 1, D_MODEL), lambda i, j: (i // per_b, k, 0, 0))
    return pl.pallas_call(
        functools.partial(_inproj_kernel, tn=tn),
        grid=(t_rows // tm, PROJ_W // tn),
        in_specs=[pl.BlockSpec((tm, D_MODEL), lambda i, j: (i, 0)),
                  mod_spec(1), mod_spec(0),
                  pl.BlockSpec((1, D_MODEL), lambda i, j: (0, 0)),
                  pl.BlockSpec((D_MODEL, tn), lambda i, j: (0, j)),
                  pl.BlockSpec((D_MODEL, DT_W), lambda i, j: (0, 0)),
                  tab_spec, tab_spec, tab_spec],
        out_specs=[pl.BlockSpec((tm, tn), lambda i, j: (i, j)),
                   pl.BlockSpec((tm, DT_W), lambda i, j: (i, 0))],
        out_shape=[jax.ShapeDtypeStruct((t_rows, PROJ_W), BF16),
                   jax.ShapeDtypeStruct((t_rows, DT_W), F32)],
        scratch_shapes=[pltpu.VMEM((tm, D_MODEL), BF16)],
        compiler_params=_cparams(("arbitrary", "arbitrary")),
        name="norm_in_proj",
    )(x2, mod4, mod4, g_pre.reshape(1, D_MODEL), w_main, w_dt, cos, slo, shi)


def _attn_stages(sink_ref, q_ref, kp_ref, kc_ref, kn_ref, vp_ref, vc_ref, vn_ref, o_ref):
    i = pl.program_id(1)
    nblk = pl.num_programs(1)
    rows2 = 2 * BLOCK
    r = lax.broadcasted_iota(jnp.int32, (rows2, BLOCK), 0) % BLOCK
    c = lax.broadcasted_iota(jnp.int32, (rows2, BLOCK), 1)
    top = lax.broadcasted_iota(jnp.int32, (rows2, 1), 0) < BLOCK
    neg = jnp.float32(-jnp.inf)
    bias_p = jnp.where((c >= r) & (i > 0), 0.0, neg)
    bias_n = jnp.where((c <= r) & (i < nblk - 1), 0.0, neg)
    lane3 = lax.broadcasted_iota(jnp.int32, (3 * BLOCK, LANES), 1)
    low3 = lane3 < HEAD_DIM
    low2 = c < HEAD_DIM

    def padded_variants(p_ref, c_ref, n_ref, slab):
        sl = slice(slab * LANES, (slab + 1) * LANES)
        x = jnp.concatenate([p_ref[:, sl], c_ref[:, sl], n_ref[:, sl]], axis=0)
        xr = pltpu.bitcast(pltpu.roll(pltpu.bitcast(x, jnp.uint32), HEAD_DIM, 1), BF16)
        zero = jnp.zeros_like(x)
        return ((jnp.where(low3, x, zero), jnp.where(low3, zero, xr)),
                (jnp.where(low3, xr, zero), jnp.where(low3, zero, x)))

    variants = {}

    def scores(g):
        slab, par = divmod(g, 2)
        if slab not in variants:
            variants[slab] = (padded_variants(kp_ref, kc_ref, kn_ref, slab),
                              padded_variants(vp_ref, vc_ref, vn_ref, slab))
        k_var, v_var = variants[slab]
        qq = jnp.concatenate([q_ref[:, (2 * g) * LANES:(2 * g + 1) * LANES],
                              q_ref[:, (2 * g + 1) * LANES:(2 * g + 2) * LANES]], axis=0)
        kk = jnp.concatenate(k_var[par], axis=0)
        s = lax.dot_general(qq, kk, (((1,), (1,)), ((), ())), preferred_element_type=F32)
        return s, jnp.concatenate(v_var[par], axis=0)

    def finish(g, s, vv):
        probs, inv = [], []
        for e in range(2):
            base = e * 3 * BLOCK
            sp = s[:, base:base + BLOCK] + bias_p
            sc = s[:, base + BLOCK:base + 2 * BLOCK]
            sn = s[:, base + 2 * BLOCK:base + 3 * BLOCK] + bias_n
            sink = jnp.where(top, sink_ref[4 * g + e], sink_ref[4 * g + 2 + e]) * LOG2E
            m = jnp.maximum(jnp.max(jnp.maximum(jnp.maximum(sp, sc), sn), axis=-1, keepdims=True), sink)
            pp, pc, pn = jnp.exp2(sp - m), jnp.exp2(sc - m), jnp.exp2(sn - m)
            denom = jnp.sum(pp + pc + pn, axis=-1, keepdims=True) + jnp.exp2(sink - m)
            probs += [pp.astype(BF16), pc.astype(BF16), pn.astype(BF16)]
            inv.append(1.0 / denom)
        o = jnp.dot(jnp.concatenate(probs, axis=1), vv, preferred_element_type=F32)
        o = o * jnp.where(low2, inv[0], inv[1])
        o_ref[:, (2 * g) * LANES:(2 * g + 1) * LANES] = o[:BLOCK].astype(BF16)
        o_ref[:, (2 * g + 1) * LANES:(2 * g + 2) * LANES] = o[BLOCK:].astype(BF16)

    return scores, finish


CONV_ROWS = 256
N_SLABS = CONV_DIM // LANES


def _conv_kernel(xp_ref, xm_ref, xn_ref, cw_ref, o_ref, in_scr, out_scr):
    i = pl.program_id(1)
    n = pl.num_programs(1)
    has_prev = (i > 0).astype(F32)
    has_next = (i < n - 1).astype(F32)
    rows = CONV_ROWS
    for cs in range(N_SLABS):
        sl = slice(cs * LANES, (cs + 1) * LANES)
        in_scr[cs, 0:HALO, :] = xp_ref[:, sl].astype(F32) * has_prev
        in_scr[cs, HALO:HALO + rows, :] = xm_ref[:, sl].astype(F32)
        in_scr[cs, HALO + rows:, :] = xn_ref[:, sl].astype(F32) * has_next

    def slab_body(cs, carry):
        cw = cw_ref[cs]
        taps = [jnp.broadcast_to(cw[w:w + 1, :], (8, LANES)) for w in range(CONV_W)]
        bias = jnp.broadcast_to(cw[CONV_W:CONV_W + 1, :], (8, LANES))
        for g in range(rows // 16):
            for e in range(2):
                acc = bias
                for w in range(CONV_W):
                    start = HALO + 16 * g + e + w - CONV_W // 2
                    acc = acc + in_scr[cs, pl.ds(start, 8, stride=2), :] * taps[w]
                out_scr[cs, pl.ds(16 * g + e, 8, stride=2), :] = _silu(acc)
        return carry

    lax.fori_loop(0, N_SLABS, slab_body, 0)
    for cs in range(N_SLABS):
        o_ref[:, cs * LANES:(cs + 1) * LANES] = out_scr[cs].astype(BF16)


def _conv_silu(proj3, conv_taps):
    batch, seq, _ = proj3.shape
    rows = CONV_ROWS
    hpr = rows // HALO
    nh = seq // HALO
    col = COL_XBC // CONV_DIM
    return pl.pallas_call(
        _conv_kernel,
        grid=(batch, seq // rows),
        in_specs=[pl.BlockSpec((None, HALO, CONV_DIM), lambda b, i: (b, jnp.maximum(i * hpr - 1, 0), col)),
                  pl.BlockSpec((None, rows, CONV_DIM), lambda b, i: (b, i, col)),
                  pl.BlockSpec((None, HALO, CONV_DIM), lambda b, i: (b, jnp.minimum((i + 1) * hpr, nh - 1), col)),
                  pl.BlockSpec((N_SLABS, 8, LANES), lambda b, i: (0, 0, 0))],
        out_specs=pl.BlockSpec((None, rows, CONV_DIM), lambda b, i: (b, i, 0)),
        out_shape=jax.ShapeDtypeStruct((batch, seq, CONV_DIM), BF16),
        scratch_shapes=[pltpu.VMEM((N_SLABS, rows + 2 * HALO, LANES), F32),
                        pltpu.VMEM((N_SLABS, rows, LANES), F32)],
        compiler_params=_cparams(("arbitrary", "arbitrary")),
        name="conv_silu",
    )(proj3, proj3, proj3, conv_taps)


def _ssd_direction(xa_ref, dt_ref, bias_ref, a_ref, e_ref, dskip_ref, st_ref, out_ref, *, reverse):
    lane0 = SSD_HEADS if reverse else 0
    xs = xa_ref[:, :D_INNER]
    bm = xa_ref[:, D_INNER:D_INNER + N_GROUPS * D_STATE]
    cm = xa_ref[:, D_INNER + N_GROUPS * D_STATE:]

    dt = jax.nn.softplus(dt_ref[...] + bias_ref[...])
    da = dt * a_ref[...]
    r = lax.broadcasted_iota(jnp.int32, (CHUNK, CHUNK), 0)
    c = lax.broadcasted_iota(jnp.int32, (CHUNK, CHUNK), 1)
    causal = (c >= r) if reverse else (c <= r)
    tri = causal.astype(BF16)
    d1 = da.astype(BF16)
    r1 = da - d1.astype(F32)
    d2 = r1.astype(BF16)
    d3 = (r1 - d2.astype(F32)).astype(BF16)
    cum = (jnp.dot(tri, d1, preferred_element_type=F32) + jnp.dot(tri, d2, preferred_element_type=F32)
           + jnp.dot(tri, d3, preferred_element_type=F32)) * LOG2E
    src_t = (cum - jnp.log2(dt)).T
    edge = CHUNK - 1 if not reverse else 0
    total = cum[edge:edge + 1, :]
    exp_cum = jnp.exp2(cum)
    decay = jnp.exp2(total - cum)

    e = e_ref[...]

    def expand(v):
        return jnp.dot(v.astype(BF16), e, preferred_element_type=F32)

    x_dec = xs * expand(dt * decay).astype(BF16)
    exp_cum_e = expand(exp_cum)
    tot8 = jnp.broadcast_to(jnp.exp2(total), (8, DT_W))
    tot_hi = tot8.astype(BF16)
    tot_lo = (tot8 - tot_hi.astype(F32)).astype(BF16)
    chunk_decay = (jnp.dot(tot_hi, e, preferred_element_type=F32)
                   + jnp.dot(tot_lo, e, preferred_element_type=F32))[0:1, :]

    heads_per_group = SSD_HEADS // N_GROUPS
    gw = heads_per_group * SSD_HEAD_DIM
    lane = lax.broadcasted_iota(jnp.int32, (CHUNK, LANES), 1)
    neg = jnp.float32(-jnp.inf)

    def group_matmuls(g):
        b_g = bm[:, g * D_STATE:(g + 1) * D_STATE]
        c_g = cm[:, g * D_STATE:(g + 1) * D_STATE]
        cb = lax.dot_general(c_g, b_g, (((1,), (1,)), ((), ())), preferred_element_type=F32)
        gs = slice(g * gw, (g + 1) * gw)
        st_prev = st_ref[:, gs]
        y_off = jnp.dot(c_g, st_prev.astype(BF16), preferred_element_type=F32)
        upd = lax.dot_general(b_g, x_dec[:, gs], (((0,), (0,)), ((), ())), preferred_element_type=F32)
        st_ref[:, gs] = st_prev * chunk_decay[:, gs] + upd
        return cb, y_off

    def group_finish(g, mm):
        cb, y_off = mm
        gs = slice(g * gw, (g + 1) * gw)
        y = y_off * exp_cum_e[:, gs]
        y_diag = []
        for pair in range(heads_per_group // 2):
            ms = []
            for hh in range(2):
                k = lane0 + g * heads_per_group + pair * 2 + hh
                seg = cum[:, k:k + 1] - src_t[k:k + 1, :]
                ms.append((cb * jnp.exp2(jnp.where(causal, seg, neg))).astype(BF16))
            col = g * gw + pair * LANES
            xp = xa_ref[:, col:col + LANES]
            zero = jnp.zeros_like(xp)
            rhs = jnp.concatenate([jnp.where(lane < SSD_HEAD_DIM, xp, zero),
                                   jnp.where(lane >= SSD_HEAD_DIM, xp, zero)], axis=0)
            y_diag.append(jnp.dot(jnp.concatenate(ms, axis=1), rhs, preferred_element_type=F32))
        y = y + jnp.concatenate(y_diag, axis=1)
        if dskip_ref is not None:
            y = y + dskip_ref[:, gs] * xs[:, gs].astype(F32)
        out_ref[:, gs] = y.astype(out_ref.dtype)

    return group_matmuls, group_finish


def _mixers_kernel(sink_ref, q_ref, kp_ref, kc_ref, kn_ref, vp_ref, vc_ref, vn_ref,
                   xa_f, dt_f, xa_b, dt_b, bias_ref, a_ref, e_f, e_b, dskip_ref,
                   attn_ref, yf_ref, yb_ref, st_f, st_b):
    @pl.when(pl.program_id(1) == 0)
    def _():
        st_f[...] = jnp.zeros_like(st_f)
        st_b[...] = jnp.zeros_like(st_b)

    scores, attn_finish = _attn_stages(sink_ref, q_ref, kp_ref, kc_ref, kn_ref, vp_ref, vc_ref, vn_ref, attn_ref)
    dirs = [_ssd_direction(xa_f, dt_f, bias_ref, a_ref, e_f, dskip_ref, st_f, yf_ref, reverse=False),
            _ssd_direction(xa_b, dt_b, bias_ref, a_ref, e_b, None, st_b, yb_ref, reverse=True)]
    assert N_GROUPS == N_KV_HEADS
    pend_a = scores(0)
    pend_s = [mm(0) for mm, _ in dirs]
    for g in range(N_GROUPS):
        last = g + 1 == N_GROUPS
        next_a = None if last else scores(g + 1)
        next_s = None if last else [mm(g + 1) for mm, _ in dirs]
        attn_finish(g, *pend_a)
        for (_, fin), mm_out in zip(dirs, pend_s):
            fin(g, mm_out)
        pend_a, pend_s = next_a, next_s


def _token_mixers(proj3, xact3, dt3, sinks, dt_bias, a_neg, e_f, e_b, dskip_e):
    batch, seq, _ = proj3.shape
    n = seq // BLOCK
    assert BLOCK == CHUNK
    kcol, vcol = COL_K // KV_WIDTH, COL_V // KV_WIDTH
    prev = lambda col: pl.BlockSpec((None, BLOCK, KV_WIDTH), lambda b, i: (b, jnp.maximum(i - 1, 0), col))
    cur = lambda col: pl.BlockSpec((None, BLOCK, KV_WIDTH), lambda b, i: (b, i, col))
    nxt = lambda col: pl.BlockSpec((None, BLOCK, KV_WIDTH), lambda b, i: (b, jnp.minimum(i + 1, n - 1), col))

    def ssd_specs(chunk_of):
        return [pl.BlockSpec((None, CHUNK, CONV_DIM), lambda b, i: (b, chunk_of(i), 0)),
                pl.BlockSpec((None, CHUNK, DT_W), lambda b, i: (b, chunk_of(i), 0))]

    fwd = lambda i: i
    bwd = lambda i: n - 1 - i
    const = lambda shape: pl.BlockSpec(shape, lambda b, i: (0,) * len(shape))
    row_f = pl.BlockSpec((None, BLOCK, D_MODEL), lambda b, i: (b, i, 0))
    row_b = pl.BlockSpec((None, BLOCK, D_MODEL), lambda b, i: (b, n - 1 - i, 0))
    shape = jax.ShapeDtypeStruct((batch, seq, D_MODEL), BF16)
    return pl.pallas_call(
        _mixers_kernel,
        grid=(batch, n),
        in_specs=[pl.BlockSpec(memory_space=pltpu.SMEM),
                  pl.BlockSpec((None, BLOCK, D_MODEL), lambda b, i: (b, i, COL_Q // D_MODEL)),
                  prev(kcol), cur(kcol), nxt(kcol), prev(vcol), cur(vcol), nxt(vcol)]
                 + ssd_specs(fwd) + ssd_specs(bwd)
                 + [const((1, DT_W)), const((1, DT_W)), const((DT_W, D_INNER)), const((DT_W, D_INNER)),
                    const((1, D_INNER))],
        out_specs=[row_f, row_f, row_b],
        out_shape=[shape, shape, shape],
        scratch_shapes=[pltpu.VMEM((D_STATE, D_INNER), F32), pltpu.VMEM((D_STATE, D_INNER), F32)],
        compiler_params=_cparams(("arbitrary", "arbitrary")),
        name="token_mixers",
    )(sinks, proj3, proj3, proj3, proj3, proj3, proj3, proj3, xact3, dt3, xact3, dt3,
      dt_bias, a_neg, e_f, e_b, dskip_e)


def _outproj_kernel(x_ref, gt_ref, yf_ref, yb_ref, z_ref, attn_ref, ga_ref, gs_ref, nw_ref, w_ref, gpost_ref,
                    o_ref):
    z = z_ref[...]
    y = (yf_ref[...].astype(F32) + yb_ref[...].astype(F32)) * (z * jax.nn.sigmoid(z)).astype(F32)
    gw = D_INNER // N_GROUPS
    parts = []
    for g in range(N_GROUPS):
        yg = y[:, g * gw:(g + 1) * gw]
        parts.append(yg * lax.rsqrt(jnp.mean(yg * yg, axis=-1, keepdims=True) + EPS))
    ssd = (jnp.concatenate(parts, axis=1) * nw_ref[...]).astype(BF16)
    merged = jax.nn.sigmoid(ga_ref[...]) * attn_ref[...] + jax.nn.sigmoid(gs_ref[...]) * ssd
    mix = jnp.dot(merged, w_ref[...], preferred_element_type=F32)
    nrm = mix * lax.rsqrt(jnp.mean(mix * mix, axis=-1, keepdims=True) + EPS) * gpost_ref[...]
    o_ref[...] = x_ref[...] + gt_ref[...] * nrm


def _out_projection(x2, mod4, yf2, yb2, proj2, attn2, ssd_norm_w, w_out, g_post, batch, seq):
    t_rows = batch * seq
    tm = min(seq, 256)
    per_b = seq // tm
    row = lambda col: pl.BlockSpec((tm, D_MODEL), lambda i: (i, col))
    vec = pl.BlockSpec((1, D_MODEL), lambda i: (0, 0))
    return pl.pallas_call(
        _outproj_kernel,
        grid=(t_rows // tm,),
        in_specs=[row(0),
                  pl.BlockSpec((None, None, 1, D_MODEL), lambda i: (i // per_b, 2, 0, 0)),
                  row(0), row(0), row(COL_Z // D_MODEL), row(0),
                  row(COL_GATE // D_MODEL), row(COL_GATE // D_MODEL + 1),
                  vec,
                  pl.BlockSpec((D_MODEL, D_MODEL), lambda i: (0, 0)),
                  vec],
        out_specs=row(0),
        out_shape=jax.ShapeDtypeStruct((t_rows, D_MODEL), F32),
        compiler_params=_cparams(("arbitrary",)),
        name="merge_out_proj",
    )(x2, mod4, yf2, yb2, proj2, attn2, proj2, proj2, ssd_norm_w.reshape(1, D_INNER), w_out,
      g_post.reshape(1, D_MODEL))


FFN_PART_ROWS = 512


def _ffn_kernel(x_ref, sc_ref, sh_ref, gt_ref, gpre_ref, wg_ref, wu_ref, wd_ref, gpost_ref, o_ref, h_scr):
    j = pl.program_id(1)

    @pl.when(j == 0)
    def _():
        _modulated_prenorm(x_ref, gpre_ref, sc_ref, sh_ref, h_scr)
        o_ref[...] = jnp.zeros_like(o_ref)

    for p in range(h_scr.shape[0] // FFN_PART_ROWS):
        rows = slice(p * FFN_PART_ROWS, (p + 1) * FFN_PART_ROWS)
        h = h_scr[rows, :]
        gate = jnp.dot(h, wg_ref[...], preferred_element_type=F32)
        up = jnp.dot(h, wu_ref[...], preferred_element_type=F32)
        act = (_silu(gate) * up).astype(BF16)
        o_ref[rows, :] += jnp.dot(act, wd_ref[...], preferred_element_type=F32)

    @pl.when(j == pl.num_programs(1) - 1)
    def _():
        _gated_postnorm_residual(o_ref, x_ref, gpost_ref, gt_ref, o_ref)


def _ffn(x2, mod4, g_pre, w_gu, w_d, g_post, batch, seq):
    t_rows = batch * seq
    tm = min(seq, 1024)
    tf = 512
    assert tm % FFN_PART_ROWS == 0
    up0 = D_FF // tf
    per_b = seq // tm
    mod_spec = lambda k: pl.BlockSpec((None, None, 1, D_MODEL), lambda i, j: (i // per_b, k, 0, 0))
    vec = pl.BlockSpec((1, D_MODEL), lambda i, j: (0, 0))
    return pl.pallas_call(
        _ffn_kernel,
        grid=(t_rows // tm, D_FF // tf),
        in_specs=[pl.BlockSpec((tm, D_MODEL), lambda i, j: (i, 0)),
                  mod_spec(4), mod_spec(3), mod_spec(5), vec,
                  pl.BlockSpec((D_MODEL, tf), lambda i, j: (0, j)),
                  pl.BlockSpec((D_MODEL, tf), lambda i, j: (0, up0 + j)),
                  pl.BlockSpec((tf, D_MODEL), lambda i, j: (j, 0)),
                  vec],
        out_specs=pl.BlockSpec((tm, D_MODEL), lambda i, j: (i, 0)),
        out_shape=jax.ShapeDtypeStruct((t_rows, D_MODEL), F32),
        scratch_shapes=[pltpu.VMEM((tm, D_MODEL), BF16)],
        compiler_params=_cparams(("arbitrary", "arbitrary")),
        name="swiglu_ffn",
    )(x2, mod4, mod4, mod4, g_pre.reshape(1, D_MODEL), w_gu, w_gu, w_d, g_post.reshape(1, D_MODEL))


def _prep_weights(w_in, conv_w, conv_b, a_log_f, a_log_b, dt_bias_f, dt_bias_b, d_skip, w_out, w_gu, w_down):
    o_q, o_k, o_v = 0, D_MODEL, D_MODEL + KV_WIDTH
    o_z = o_v + KV_WIDTH
    o_xbc = o_z + D_INNER
    o_dt = o_xbc + CONV_DIM
    o_gate = o_dt + 2 * SSD_HEADS
    seg = lambda off, width: w_in[:, off:off + width]
    w_main = jnp.concatenate([seg(o_xbc, CONV_DIM), seg(o_gate, 2 * D_MODEL), seg(o_q, D_MODEL),
                              seg(o_z, D_INNER), seg(o_k, KV_WIDTH), seg(o_v, KV_WIDTH)], axis=1).astype(BF16)
    w_dt = jnp.pad(seg(o_dt, 2 * SSD_HEADS), ((0, 0), (0, DT_W - 2 * SSD_HEADS))).astype(BF16)
    pad_l = lambda v: jnp.pad(v, (0, DT_W - v.shape[0])).reshape(1, DT_W)
    dt_bias = pad_l(jnp.concatenate([dt_bias_f, dt_bias_b]))
    a_neg = pad_l(-jnp.exp(jnp.concatenate([a_log_f, a_log_b])))
    taps8 = jnp.concatenate([conv_w, conv_b[None, :], jnp.zeros((8 - CONV_W - 1, CONV_DIM), F32)], axis=0)
    conv_taps = taps8.reshape(8, N_SLABS, LANES).transpose(1, 0, 2)
    head_of_lane = jnp.arange(D_INNER) // SSD_HEAD_DIM
    rows = jnp.arange(DT_W)[:, None]
    e_f = (rows == head_of_lane[None, :]).astype(BF16)
    e_b = (rows == head_of_lane[None, :] + SSD_HEADS).astype(BF16)
    dskip_e = jnp.repeat(d_skip, SSD_HEAD_DIM).reshape(1, D_INNER)
    return dict(w_main=w_main, w_dt=w_dt, dt_bias=dt_bias, a_neg=a_neg, conv_taps=conv_taps,
                e_f=e_f, e_b=e_b, dskip_e=dskip_e,
                w_out=w_out.astype(BF16), w_gu=w_gu.astype(BF16),
                w_d=w_down.astype(BF16))


def _encoder_layer(x, mod4, tables, pw, g_pre1, g_post1, ssd_norm_w, sinks, g_pre2, g_post2):
    batch, seq, _ = x.shape
    x2 = x.reshape(batch * seq, D_MODEL)
    proj2, dt2 = _in_projection(x2, mod4, g_pre1, pw["w_main"], pw["w_dt"], tables, batch, seq)
    proj3 = proj2.reshape(batch, seq, PROJ_W)
    dt3 = dt2.reshape(batch, seq, DT_W)
    xact3 = _conv_silu(proj3, pw["conv_taps"])
    attn, yf, yb = _token_mixers(proj3, xact3, dt3, sinks, pw["dt_bias"], pw["a_neg"], pw["e_f"], pw["e_b"],
                                 pw["dskip_e"])
    x1 = _out_projection(x2, mod4, yf.reshape(batch * seq, D_INNER), yb.reshape(batch * seq, D_INNER), proj2,
                         attn.reshape(batch * seq, D_MODEL), ssd_norm_w, pw["w_out"], g_post1, batch, seq)
    out = _ffn(x1, mod4, g_pre2, pw["w_gu"], pw["w_d"], g_post2, batch, seq)
    return out.reshape(batch, seq, D_MODEL)


def kernel(x_prompt, x_sample, c_prompt, c_sample, w_ada, b_ada, g_pre1, g_post1, w_in, conv_w, conv_b, a_log_f,
           a_log_b, dt_bias_f, dt_bias_b, d_skip, ssd_norm_w, sinks, w_out, g_pre2, g_post2, w_gu, w_down):
    assert w_ada.shape[0] == 1, "single-layer problem"
    nb_p, nb_s = c_prompt.shape[0], c_sample.shape[0]
    nb = -(-(nb_p + nb_s) // 8) * 8
    c_all = jnp.pad(jnp.concatenate([c_prompt, c_sample], axis=0), ((0, nb - nb_p - nb_s), (0, 0)))
    mod = _modulation(c_all, w_ada[0], b_ada[0]).reshape(nb, 6, 1, D_MODEL)
    pw = _prep_weights(w_in[0], conv_w[0], conv_b[0], a_log_f[0], a_log_b[0], dt_bias_f[0], dt_bias_b[0],
                       d_skip[0], w_out[0], w_gu[0], w_down[0])
    outs = []
    for x, mod4 in ((x_prompt, mod[:nb_p]), (x_sample, mod[nb_p:nb_p + nb_s])):
        tables = _rope_tables(x.shape[1])
        outs.append(_encoder_layer(x, mod4, tables, pw, g_pre1[0], g_post1[0], ssd_norm_w[0], sinks[0],
                                   g_pre2[0], g_post2[0]))
    return tuple(outs)
```

```python
import functools
import math

import jax
import jax.numpy as jnp
from jax import lax
from jax.experimental import pallas as pl
from jax.experimental.pallas import tpu as pltpu

F32 = jnp.float32
BF16 = jnp.bfloat16

D_MODEL = 2048
N_HEADS = 32
N_KV_HEADS = 8
HEAD_DIM = 64
KV_WIDTH = N_KV_HEADS * HEAD_DIM
BLOCK = 128
ROPE_THETA = 500000.0
ROPE_DIM = HEAD_DIM // 4
SSD_HEADS = 32
SSD_HEAD_DIM = 64
D_INNER = SSD_HEADS * SSD_HEAD_DIM
D_STATE = 128
N_GROUPS = 8
CONV_W = 5
CONV_DIM = D_INNER + 2 * N_GROUPS * D_STATE
CHUNK = 128
D_FF = 5632
EPS = 1e-6
LOG2E = math.log2(math.e)

LANES = 128
HALO = 16

COL_XBC = 0
COL_GATE = COL_XBC + CONV_DIM
COL_Q = COL_GATE + 2 * D_MODEL
COL_Z = COL_Q + D_MODEL
COL_K = COL_Z + D_INNER
COL_V = COL_K + KV_WIDTH
PROJ_W = COL_V + KV_WIDTH
DT_W = LANES

VMEM_LIMIT = 56 * 1024 * 1024


def _cparams(sem):
    return pltpu.CompilerParams(dimension_semantics=sem, vmem_limit_bytes=VMEM_LIMIT)


def _silu(v):
    return v * jax.nn.sigmoid(v)


NORM_ROWS = 16


def _rowwise_rms_apply(src_ref, apply):
    n_rows, width = src_ref.shape
    v = src_ref[...]
    inv = lax.rsqrt(jnp.sum(v * v, axis=-1, keepdims=True) * (1.0 / width) + EPS)
    for r in range(n_rows // NORM_ROWS):
        rows = slice(r * NORM_ROWS, (r + 1) * NORM_ROWS)
        apply(rows, src_ref[rows, :], inv[rows])


def _modulated_prenorm(x_ref, g_ref, sc_ref, sh_ref, h_ref):
    tile = (NORM_ROWS, x_ref.shape[1])
    gmod = jnp.broadcast_to(g_ref[...] * (1.0 + sc_ref[...]), tile)
    shift = jnp.broadcast_to(sh_ref[...], tile)

    def apply(rows, x, inv):
        h_ref[rows, :] = (x * inv * gmod + shift).astype(BF16)

    _rowwise_rms_apply(x_ref, apply)


def _gated_postnorm_residual(f_ref, x_ref, gpost_ref, gt_ref, o_ref):
    gg = jnp.broadcast_to(gpost_ref[...] * gt_ref[...], (NORM_ROWS, f_ref.shape[1]))

    def apply(rows, f, inv):
        o_ref[rows, :] = x_ref[rows, :] + f * inv * gg

    _rowwise_rms_apply(f_ref, apply)


def _mod_kernel(c_ref, w_ref, b_ref, o_ref):
    c = c_ref[...]
    s = _silu(c).astype(BF16)
    o_ref[...] = jnp.dot(s, w_ref[...].astype(BF16), preferred_element_type=F32) + b_ref[...]


def _modulation(c_all, w_ada, b_ada):
    nb = c_all.shape[0]
    n = w_ada.shape[1]
    tn = 1024
    return pl.pallas_call(
        _mod_kernel,
        grid=(n // tn,),
        in_specs=[pl.BlockSpec((nb, D_MODEL), lambda j: (0, 0)),
                  pl.BlockSpec((D_MODEL, tn), lambda j: (0, j)),
                  pl.BlockSpec((1, tn), lambda j: (0, j))],
        out_specs=pl.BlockSpec((nb, tn), lambda j: (0, j)),
        out_shape=jax.ShapeDtypeStruct((nb, n), F32),
        compiler_params=_cparams(("arbitrary",)),
        name="adaln_mod",
    )(c_all, w_ada, b_ada.reshape(1, n))


def _rope_table_kernel(cos_ref, sin_lo_ref, sin_hi_ref):
    tm = cos_ref.shape[0]
    half = ROPE_DIM // 2
    pos = (pl.program_id(0) * tm + lax.broadcasted_iota(jnp.int32, (tm, LANES), 0)).astype(F32)
    d = lax.broadcasted_iota(jnp.int32, (tm, LANES), 1) % HEAD_DIM
    fidx = (d % half).astype(F32)
    inv = jnp.exp(fidx * (-2.0 / ROPE_DIM * math.log(ROPE_THETA)))
    ang = pos * inv
    cos = jnp.cos(ang)
    sin = jnp.sin(ang)
    cos_ref[...] = jnp.where(d < ROPE_DIM, cos, 1.0)
    sin_lo_ref[...] = jnp.where(d < half, -sin, 0.0)
    sin_hi_ref[...] = jnp.where((d >= half) & (d < ROPE_DIM), sin, 0.0)


def _rope_tables(seq):
    tm = min(seq, 2048)
    shp = jax.ShapeDtypeStruct((seq, LANES), F32)
    spec = pl.BlockSpec((tm, LANES), lambda i: (i, 0))
    return pl.pallas_call(
        _rope_table_kernel,
        grid=(seq // tm,),
        in_specs=[],
        out_specs=[spec, spec, spec],
        out_shape=[shp, shp, shp],
        compiler_params=_cparams(("arbitrary",)),
        name="rope_tables",
    )()


def _rope_slab(t, cos, sin_lo, sin_hi):
    half = ROPE_DIM // 2
    return t * cos + pltpu.roll(t, LANES - half, 1) * sin_lo + pltpu.roll(t, half, 1) * sin_hi


def _inproj_kernel(x_ref, sc_ref, sh_ref, g_ref, w_ref, wdt_ref, cos_ref, slo_ref, shi_ref,
                   proj_ref, dt_ref, h_scr, *, tn):
    j = pl.program_id(1)

    @pl.when(j == 0)
    def _():
        _modulated_prenorm(x_ref, g_ref, sc_ref, sh_ref, h_scr)
        dt_ref[...] = jnp.dot(h_scr[...], wdt_ref[...], preferred_element_type=F32)

    q_lo, q_hi = COL_Q // tn, (COL_Q + D_MODEL) // tn
    k_tile = COL_K // tn
    k_slabs = KV_WIDTH // LANES
    is_q = (j >= q_lo) & (j < q_hi)
    is_k = j == k_tile
    tm = h_scr.shape[0]
    n_parts = 2
    rows = tm // n_parts

    def project(n_rope_slabs, scale):
        for p in range(n_parts):
            rs = slice(p * rows, (p + 1) * rows)
            acc = jnp.dot(h_scr[rs, :], w_ref[...], preferred_element_type=F32)
            if n_rope_slabs == 0:
                proj_ref[rs, :] = acc.astype(BF16)
                continue
            cos, slo, shi = cos_ref[rs, :], slo_ref[rs, :], shi_ref[rs, :]
            for s in range(tn // LANES):
                t = acc[:, s * LANES:(s + 1) * LANES]
                if s < n_rope_slabs:
                    t = _rope_slab(t, cos, slo, shi)
                    if scale != 1.0:
                        t = t * scale
                proj_ref[rs, s * LANES:(s + 1) * LANES] = t.astype(BF16)

    @pl.when(is_q)
    def _():
        project(tn // LANES, HEAD_DIM ** -0.5 * LOG2E)

    @pl.when(is_k)
    def _():
        project(k_slabs, 1.0)

    @pl.when(jnp.logical_not(is_q | is_k))
    def _():
        project(0, 1.0)


def _in_projection(x2, mod4, g_pre, w_main, w_dt, tables, batch, seq):
    t_rows = batch * seq
    tm = min(seq, 1024)
    tn = 1024
    assert COL_K % tn == 0 and COL_Q % tn == 0 and D_MODEL % tn == 0 and PROJ_W % tn == 0
    per_b = seq // tm
    cos, slo, shi = tables
    tab_spec = pl.BlockSpec((tm, LANES), lambda i, j: (i % per_b, 0))
    mod_spec = lambda k: pl.BlockSpec((None, None, 1, D_MODEL), lambda i, j: (i // per_b, k, 0, 0))
    return pl.pallas_call(
        functools.partial(_inproj_kernel, tn=tn),
        grid=(t_rows // tm, PROJ_W // tn),
        in_specs=[pl.BlockSpec((tm, D_MODEL), lambda i, j: (i, 0)),
                  mod_spec(1), mod_spec(0),
                  pl.BlockSpec((1, D_MODEL), lambda i, j: (0, 0)),
                  pl.BlockSpec((D_MODEL, tn), lambda i, j: (0, j)),
                  pl.BlockSpec((D_MODEL, DT_W), lambda i, j: (0, 0)),
                  tab_spec, tab_spec, tab_spec],
        out_specs=[pl.BlockSpec((tm, tn), lambda i, j: (i, j)),
                   pl.BlockSpec((tm, DT_W), lambda i, j: (i, 0))],
        out_shape=[jax.ShapeDtypeStruct((t_rows, PROJ_W), BF16),
                   jax.ShapeDtypeStruct((t_rows, DT_W), F32)],
        scratch_shapes=[pltpu.VMEM((tm, D_MODEL), BF16)],
        compiler_params=_cparams(("arbitrary", "arbitrary")),
        name="norm_in_proj",
    )(x2, mod4, mod4, g_pre.reshape(1, D_MODEL), w_main, w_dt, cos, slo, shi)


def _attn_stages(sink_ref, q_ref, kp_ref, kc_ref, kn_ref, vp_ref, vc_ref, vn_ref, o_ref):
    i = pl.program_id(1)
    nblk = pl.num_programs(1)
    rows2 = 2 * BLOCK
    r = lax.broadcasted_iota(jnp.int32, (rows2, BLOCK), 0) % BLOCK
    c = lax.broadcasted_iota(jnp.int32, (rows2, BLOCK), 1)
    top = lax.broadcasted_iota(jnp.int32, (rows2, 1), 0) < BLOCK
    neg = jnp.float32(-jnp.inf)
    bias_p = jnp.where((c >= r) & (i > 0), 0.0, neg)
    bias_n = jnp.where((c <= r) & (i < nblk - 1), 0.0, neg)
    lane3 = lax.broadcasted_iota(jnp.int32, (3 * BLOCK, LANES), 1)
    low3 = lane3 < HEAD_DIM
    low2 = c < HEAD_DIM

    def padded_variants(p_ref, c_ref, n_ref, slab):
        sl = slice(slab * LANES, (slab + 1) * LANES)
        x = jnp.concatenate([p_ref[:, sl], c_ref[:, sl], n_ref[:, sl]], axis=0)
        xr = pltpu.bitcast(pltpu.roll(pltpu.bitcast(x, jnp.uint32), HEAD_DIM, 1), BF16)
        zero = jnp.zeros_like(x)
        return ((jnp.where(low3, x, zero), jnp.where(low3, zero, xr)),
                (jnp.where(low3, xr, zero), jnp.where(low3, zero, x)))

    variants = {}

    def scores(g):
        slab, par = divmod(g, 2)
        if slab not in variants:
            variants[slab] = (padded_variants(kp_ref, kc_ref, kn_ref, slab),
                              padded_variants(vp_ref, vc_ref, vn_ref, slab))
        k_var, v_var = variants[slab]
        qq = jnp.concatenate([q_ref[:, (2 * g) * LANES:(2 * g + 1) * LANES],
                              q_ref[:, (2 * g + 1) * LANES:(2 * g + 2) * LANES]], axis=0)
        kk = jnp.concatenate(k_var[par], axis=0)
        s = lax.dot_general(qq, kk, (((1,), (1,)), ((), ())), preferred_element_type=F32)
        return s, jnp.concatenate(v_var[par], axis=0)

    def finish(g, s, vv):
        probs, inv = [], []
        for e in range(2):
            base = e * 3 * BLOCK
            sp = s[:, base:base + BLOCK] + bias_p
            sc = s[:, base + BLOCK:base + 2 * BLOCK]
            sn = s[:, base + 2 * BLOCK:base + 3 * BLOCK] + bias_n
            sink = jnp.where(top, sink_ref[4 * g + e], sink_ref[4 * g + 2 + e]) * LOG2E
            m = jnp.maximum(jnp.max(jnp.maximum(jnp.maximum(sp, sc), sn), axis=-1, keepdims=True), sink)
            pp, pc, pn = jnp.exp2(sp - m), jnp.exp2(sc - m), jnp.exp2(sn - m)
            denom = jnp.sum(pp + pc + pn, axis=-1, keepdims=True) + jnp.exp2(sink - m)
            probs += [pp.astype(BF16), pc.astype(BF16), pn.astype(BF16)]
            inv.append(1.0 / denom)
        o = jnp.dot(jnp.concatenate(probs, axis=1), vv, preferred_element_type=F32)
        o = o * jnp.where(low2, inv[0], inv[1])
        o_ref[:, (2 * g) * LANES:(2 * g + 1) * LANES] = o[:BLOCK].astype(BF16)
        o_ref[:, (2 * g + 1) * LANES:(2 * g + 2) * LANES] = o[BLOCK:].astype(BF16)

    return scores, finish


CONV_ROWS = 256
N_SLABS = CONV_DIM // LANES


def _conv_kernel(xp_ref, xm_ref, xn_ref, cw_ref, o_ref, in_scr, out_scr):
    i = pl.program_id(1)
    n = pl.num_programs(1)
    has_prev = (i > 0).astype(F32)
    has_next = (i < n - 1).astype(F32)
    rows = CONV_ROWS
    for cs in range(N_SLABS):
        sl = slice(cs * LANES, (cs + 1) * LANES)
        in_scr[cs, 0:HALO, :] = xp_ref[:, sl].astype(F32) * has_prev
        in_scr[cs, HALO:HALO + rows, :] = xm_ref[:, sl].astype(F32)
        in_scr[cs, HALO + rows:, :] = xn_ref[:, sl].astype(F32) * has_next

    def slab_body(cs, carry):
        cw = cw_ref[cs]
        taps = [jnp.broadcast_to(cw[w:w + 1, :], (8, LANES)) for w in range(CONV_W)]
        bias = jnp.broadcast_to(cw[CONV_W:CONV_W + 1, :], (8, LANES))
        for g in range(rows // 16):
            for e in range(2):
                acc = bias
                for w in range(CONV_W):
                    start = HALO + 16 * g + e + w - CONV_W // 2
                    acc = acc + in_scr[cs, pl.ds(start, 8, stride=2), :] * taps[w]
                half = 0.5 * acc
                out_scr[cs, pl.ds(16 * g + e, 8, stride=2), :] = half + half * jnp.tanh(half)
        return carry

    lax.fori_loop(0, N_SLABS, slab_body, 0)
    for cs in range(N_SLABS):
        o_ref[:, cs * LANES:(cs + 1) * LANES] = out_scr[cs].astype(BF16)


def _conv_silu(proj3, conv_taps):
    batch, seq, _ = proj3.shape
    rows = CONV_ROWS
    hpr = rows // HALO
    nh = seq // HALO
    col = COL_XBC // CONV_DIM
    return pl.pallas_call(
        _conv_kernel,
        grid=(batch, seq // rows),
        in_specs=[pl.BlockSpec((None, HALO, CONV_DIM), lambda b, i: (b, jnp.maximum(i * hpr - 1, 0), col)),
                  pl.BlockSpec((None, rows, CONV_DIM), lambda b, i: (b, i, col)),
                  pl.BlockSpec((None, HALO, CONV_DIM), lambda b, i: (b, jnp.minimum((i + 1) * hpr, nh - 1), col)),
                  pl.BlockSpec((N_SLABS, 8, LANES), lambda b, i: (0, 0, 0))],
        out_specs=pl.BlockSpec((None, rows, CONV_DIM), lambda b, i: (b, i, 0)),
        out_shape=jax.ShapeDtypeStruct((batch, seq, CONV_DIM), BF16),
        scratch_shapes=[pltpu.VMEM((N_SLABS, rows + 2 * HALO, LANES), F32),
                        pltpu.VMEM((N_SLABS, rows, LANES), F32)],
        compiler_params=_cparams(("arbitrary", "arbitrary")),
        name="conv_silu",
    )(proj3, proj3, proj3, conv_taps)


def _ssd_direction(xa_ref, dt_ref, bias_ref, a_ref, e_ref, dskip_ref, st_ref, out_ref, *, reverse):
    lane0 = SSD_HEADS if reverse else 0
    xs = xa_ref[:, :D_INNER]
    bm = xa_ref[:, D_INNER:D_INNER + N_GROUPS * D_STATE]
    cm = xa_ref[:, D_INNER + N_GROUPS * D_STATE:]

    dt = jax.nn.softplus(dt_ref[...] + bias_ref[...])
    da = dt * a_ref[...]
    r = lax.broadcasted_iota(jnp.int32, (CHUNK, CHUNK), 0)
    c = lax.broadcasted_iota(jnp.int32, (CHUNK, CHUNK), 1)
    causal = (c >= r) if reverse else (c <= r)
    tri = causal.astype(BF16)
    d1 = da.astype(BF16)
    r1 = da - d1.astype(F32)
    d2 = r1.astype(BF16)
    d3 = (r1 - d2.astype(F32)).astype(BF16)
    cum = (jnp.dot(tri, d1, preferred_element_type=F32) + jnp.dot(tri, d2, preferred_element_type=F32)
           + jnp.dot(tri, d3, preferred_element_type=F32)) * LOG2E
    src_t = (cum - jnp.log2(dt)).T
    edge = CHUNK - 1 if not reverse else 0
    total = cum[edge:edge + 1, :]
    exp_cum = jnp.exp2(cum)
    decay = jnp.exp2(total - cum)

    e = e_ref[...]

    def expand(v):
        return jnp.dot(v.astype(BF16), e, preferred_element_type=F32)

    x_dec = xs * expand(dt * decay).astype(BF16)
    exp_cum_e = expand(exp_cum)
    tot8 = jnp.broadcast_to(jnp.exp2(total), (8, DT_W))
    tot_hi = tot8.astype(BF16)
    tot_lo = (tot8 - tot_hi.astype(F32)).astype(BF16)
    chunk_decay = (jnp.dot(tot_hi, e, preferred_element_type=F32)
                   + jnp.dot(tot_lo, e, preferred_element_type=F32))[0:1, :]

    heads_per_group = SSD_HEADS // N_GROUPS
    gw = heads_per_group * SSD_HEAD_DIM
    lane = lax.broadcasted_iota(jnp.int32, (CHUNK, LANES), 1)
    neg = jnp.float32(-jnp.inf)

    def group_matmuls(g):
        b_g = bm[:, g * D_STATE:(g + 1) * D_STATE]
        c_g = cm[:, g * D_STATE:(g + 1) * D_STATE]
        cb = lax.dot_general(c_g, b_g, (((1,), (1,)), ((), ())), preferred_element_type=F32)
        gs = slice(g * gw, (g + 1) * gw)
        st_prev = st_ref[:, gs]
        y_off = jnp.dot(c_g, st_prev.astype(BF16), preferred_element_type=F32)
        upd = lax.dot_general(b_g, x_dec[:, gs], (((0,), (0,)), ((), ())), preferred_element_type=F32)
        st_ref[:, gs] = st_prev * chunk_decay[:, gs] + upd
        return cb, y_off

    def group_finish(g, mm):
        cb, y_off = mm
        gs = slice(g * gw, (g + 1) * gw)
        y = y_off * exp_cum_e[:, gs]
        y_diag = []
        for pair in range(heads_per_group // 2):
            ms = []
            for hh in range(2):
                k = lane0 + g * heads_per_group + pair * 2 + hh
                seg = cum[:, k:k + 1] - src_t[k:k + 1, :]
                ms.append((cb * jnp.exp2(jnp.where(causal, seg, neg))).astype(BF16))
            col = g * gw + pair * LANES
            xp = xa_ref[:, col:col + LANES]
            zero = jnp.zeros_like(xp)
            rhs = jnp.concatenate([jnp.where(lane < SSD_HEAD_DIM, xp, zero),
                                   jnp.where(lane >= SSD_HEAD_DIM, xp, zero)], axis=0)
            y_diag.append(jnp.dot(jnp.concatenate(ms, axis=1), rhs, preferred_element_type=F32))
        y = y + jnp.concatenate(y_diag, axis=1)
        if dskip_ref is not None:
            y = y + dskip_ref[:, gs] * xs[:, gs].astype(F32)
        out_ref[:, gs] = y.astype(out_ref.dtype)

    return group_matmuls, group_finish


def _mixers_kernel(sink_ref, q_ref, kp_ref, kc_ref, kn_ref, vp_ref, vc_ref, vn_ref,
                   xa_f, dt_f, xa_b, dt_b, bias_ref, a_ref, e_f, e_b, dskip_ref,
                   attn_ref, yf_ref, yb_ref, st_f, st_b):
    @pl.when(pl.program_id(1) == 0)
    def _():
        st_f[...] = jnp.zeros_like(st_f)
        st_b[...] = jnp.zeros_like(st_b)

    scores, attn_finish = _attn_stages(sink_ref, q_ref, kp_ref, kc_ref, kn_ref, vp_ref, vc_ref, vn_ref, attn_ref)
    dirs = [_ssd_direction(xa_f, dt_f, bias_ref, a_ref, e_f, dskip_ref, st_f, yf_ref, reverse=False),
            _ssd_direction(xa_b, dt_b, bias_ref, a_ref, e_b, None, st_b, yb_ref, reverse=True)]
    assert N_GROUPS == N_KV_HEADS
    pend_a = scores(0)
    pend_s = [mm(0) for mm, _ in dirs]
    for g in range(N_GROUPS):
        last = g + 1 == N_GROUPS
        next_a = None if last else scores(g + 1)
        next_s = None if last else [mm(g + 1) for mm, _ in dirs]
        attn_finish(g, *pend_a)
        for (_, fin), mm_out in zip(dirs, pend_s):
            fin(g, mm_out)
        pend_a, pend_s = next_a, next_s


def _token_mixers(proj3, xact3, dt3, sinks, dt_bias, a_neg, e_f, e_b, dskip_e):
    batch, seq, _ = proj3.shape
    n = seq // BLOCK
    assert BLOCK == CHUNK
    kcol, vcol = COL_K // KV_WIDTH, COL_V // KV_WIDTH
    prev = lambda col: pl.BlockSpec((None, BLOCK, KV_WIDTH), lambda b, i: (b, jnp.maximum(i - 1, 0), col))
    cur = lambda col: pl.BlockSpec((None, BLOCK, KV_WIDTH), lambda b, i: (b, i, col))
    nxt = lambda col: pl.BlockSpec((None, BLOCK, KV_WIDTH), lambda b, i: (b, jnp.minimum(i + 1, n - 1), col))

    def ssd_specs(chunk_of):
        return [pl.BlockSpec((None, CHUNK, CONV_DIM), lambda b, i: (b, chunk_of(i), 0)),
                pl.BlockSpec((None, CHUNK, DT_W), lambda b, i: (b, chunk_of(i), 0))]

    fwd = lambda i: i
    bwd = lambda i: n - 1 - i
    const = lambda shape: pl.BlockSpec(shape, lambda b, i: (0,) * len(shape))
    row_f = pl.BlockSpec((None, BLOCK, D_MODEL), lambda b, i: (b, i, 0))
    row_b = pl.BlockSpec((None, BLOCK, D_MODEL), lambda b, i: (b, n - 1 - i, 0))
    shape = jax.ShapeDtypeStruct((batch, seq, D_MODEL), BF16)
    return pl.pallas_call(
        _mixers_kernel,
        grid=(batch, n),
        in_specs=[pl.BlockSpec(memory_space=pltpu.SMEM),
                  pl.BlockSpec((None, BLOCK, D_MODEL), lambda b, i: (b, i, COL_Q // D_MODEL)),
                  prev(kcol), cur(kcol), nxt(kcol), prev(vcol), cur(vcol), nxt(vcol)]
                 + ssd_specs(fwd) + ssd_specs(bwd)
                 + [const((1, DT_W)), const((1, DT_W)), const((DT_W, D_INNER)), const((DT_W, D_INNER)),
                    const((1, D_INNER))],
        out_specs=[row_f, row_f, row_b],
        out_shape=[shape, shape, shape],
        scratch_shapes=[pltpu.VMEM((D_STATE, D_INNER), F32), pltpu.VMEM((D_STATE, D_INNER), F32)],
        compiler_params=_cparams(("arbitrary", "arbitrary")),
        name="token_mixers",
    )(sinks, proj3, proj3, proj3, proj3, proj3, proj3, proj3, xact3, dt3, xact3, dt3,
      dt_bias, a_neg, e_f, e_b, dskip_e)


def _outproj_kernel(x_ref, gt_ref, yf_ref, yb_ref, z_ref, attn_ref, ga_ref, gs_ref, nw_ref, w_ref, gpost_ref,
                    o_ref):
    z = z_ref[...]
    y = (yf_ref[...].astype(F32) + yb_ref[...].astype(F32)) * (z * jax.nn.sigmoid(z)).astype(F32)
    gw = D_INNER // N_GROUPS
    parts = []
    for g in range(N_GROUPS):
        yg = y[:, g * gw:(g + 1) * gw]
        parts.append(yg * lax.rsqrt(jnp.mean(yg * yg, axis=-1, keepdims=True) + EPS))
    ssd = (jnp.concatenate(parts, axis=1) * nw_ref[...]).astype(BF16)
    merged = jax.nn.sigmoid(ga_ref[...]) * attn_ref[...] + jax.nn.sigmoid(gs_ref[...]) * ssd
    mix = jnp.dot(merged, w_ref[...], preferred_element_type=F32)
    nrm = mix * lax.rsqrt(jnp.mean(mix * mix, axis=-1, keepdims=True) + EPS) * gpost_ref[...]
    o_ref[...] = x_ref[...] + gt_ref[...] * nrm


def _out_projection(x2, mod4, yf2, yb2, proj2, attn2, ssd_norm_w, w_out, g_post, batch, seq):
    t_rows = batch * seq
    tm = min(seq, 256)
    per_b = seq // tm
    row = lambda col: pl.BlockSpec((tm, D_MODEL), lambda i: (i, col))
    vec = pl.BlockSpec((1, D_MODEL), lambda i: (0, 0))
    return pl.pallas_call(
        _outproj_kernel,
        grid=(t_rows // tm,),
        in_specs=[row(0),
                  pl.BlockSpec((None, None, 1, D_MODEL), lambda i: (i // per_b, 2, 0, 0)),
                  row(0), row(0), row(COL_Z // D_MODEL), row(0),
                  row(COL_GATE // D_MODEL), row(COL_GATE // D_MODEL + 1),
                  vec,
                  pl.BlockSpec((D_MODEL, D_MODEL), lambda i: (0, 0)),
                  vec],
        out_specs=row(0),
        out_shape=jax.ShapeDtypeStruct((t_rows, D_MODEL), F32),
        compiler_params=_cparams(("arbitrary",)),
        name="merge_out_proj",
    )(x2, mod4, yf2, yb2, proj2, attn2, proj2, proj2, ssd_norm_w.reshape(1, D_INNER), w_out,
      g_post.reshape(1, D_MODEL))


FFN_PART_ROWS = 512


def _ffn_kernel(x_ref, sc_ref, sh_ref, gt_ref, gpre_ref, wg_ref, wu_ref, wd_ref, gpost_ref, o_ref, h_scr):
    j = pl.program_id(1)

    @pl.when(j == 0)
    def _():
        _modulated_prenorm(x_ref, gpre_ref, sc_ref, sh_ref, h_scr)
        o_ref[...] = jnp.zeros_like(o_ref)

    for p in range(h_scr.shape[0] // FFN_PART_ROWS):
        rows = slice(p * FFN_PART_ROWS, (p + 1) * FFN_PART_ROWS)
        h = h_scr[rows, :]
        gate = jnp.dot(h, wg_ref[...], preferred_element_type=F32)
        up = jnp.dot(h, wu_ref[...], preferred_element_type=F32)
        act = (_silu(gate) * up).astype(BF16)
        o_ref[rows, :] += jnp.dot(act, wd_ref[...], preferred_element_type=F32)

    @pl.when(j == pl.num_programs(1) - 1)
    def _():
        _gated_postnorm_residual(o_ref, x_ref, gpost_ref, gt_ref, o_ref)


def _ffn(x2, mod4, g_pre, w_gu, w_d, g_post, batch, seq):
    t_rows = batch * seq
    tm = min(seq, 1024)
    tf = 512
    assert tm % FFN_PART_ROWS == 0
    up0 = D_FF // tf
    per_b = seq // tm
    mod_spec = lambda k: pl.BlockSpec((None, None, 1, D_MODEL), lambda i, j: (i // per_b, k, 0, 0))
    vec = pl.BlockSpec((1, D_MODEL), lambda i, j: (0, 0))
    return pl.pallas_call(
        _ffn_kernel,
        grid=(t_rows // tm, D_FF // tf),
        in_specs=[pl.BlockSpec((tm, D_MODEL), lambda i, j: (i, 0)),
                  mod_spec(4), mod_spec(3), mod_spec(5), vec,
                  pl.BlockSpec((D_MODEL, tf), lambda i, j: (0, j)),
                  pl.BlockSpec((D_MODEL, tf), lambda i, j: (0, up0 + j)),
                  pl.BlockSpec((tf, D_MODEL), lambda i, j: (j, 0)),
                  vec],
        out_specs=pl.BlockSpec((tm, D_MODEL), lambda i, j: (i, 0)),
        out_shape=jax.ShapeDtypeStruct((t_rows, D_MODEL), F32),
        scratch_shapes=[pltpu.VMEM((tm, D_MODEL), BF16)],
        compiler_params=_cparams(("arbitrary", "arbitrary")),
        name="swiglu_ffn",
    )(x2, mod4, mod4, mod4, g_pre.reshape(1, D_MODEL), w_gu, w_gu, w_d, g_post.reshape(1, D_MODEL))


def _prep_weights(w_in, conv_w, conv_b, a_log_f, a_log_b, dt_bias_f, dt_bias_b, d_skip, w_out, w_gu, w_down):
    o_q, o_k, o_v = 0, D_MODEL, D_MODEL + KV_WIDTH
    o_z = o_v + KV_WIDTH
    o_xbc = o_z + D_INNER
    o_dt = o_xbc + CONV_DIM
    o_gate = o_dt + 2 * SSD_HEADS
    seg = lambda off, width: w_in[:, off:off + width]
    w_main = jnp.concatenate([seg(o_xbc, CONV_DIM), seg(o_gate, 2 * D_MODEL), seg(o_q, D_MODEL),
                              seg(o_z, D_INNER), seg(o_k, KV_WIDTH), seg(o_v, KV_WIDTH)], axis=1).astype(BF16)
    w_dt = jnp.pad(seg(o_dt, 2 * SSD_HEADS), ((0, 0), (0, DT_W - 2 * SSD_HEADS))).astype(BF16)
    pad_l = lambda v: jnp.pad(v, (0, DT_W - v.shape[0])).reshape(1, DT_W)
    dt_bias = pad_l(jnp.concatenate([dt_bias_f, dt_bias_b]))
    a_neg = pad_l(-jnp.exp(jnp.concatenate([a_log_f, a_log_b])))
    taps8 = jnp.concatenate([conv_w, conv_b[None, :], jnp.zeros((8 - CONV_W - 1, CONV_DIM), F32)], axis=0)
    conv_taps = taps8.reshape(8, N_SLABS, LANES).transpose(1, 0, 2)
    head_of_lane = jnp.arange(D_INNER) // SSD_HEAD_DIM
    rows = jnp.arange(DT_W)[:, None]
    e_f = (rows == head_of_lane[None, :]).astype(BF16)
    e_b = (rows == head_of_lane[None, :] + SSD_HEADS).astype(BF16)
    dskip_e = jnp.repeat(d_skip, SSD_HEAD_DIM).reshape(1, D_INNER)
    return dict(w_main=w_main, w_dt=w_dt, dt_bias=dt_bias, a_neg=a_neg, conv_taps=conv_taps,
                e_f=e_f, e_b=e_b, dskip_e=dskip_e,
                w_out=w_out.astype(BF16), w_gu=w_gu.astype(BF16),
                w_d=w_down.astype(BF16))


def _encoder_layer(x, mod4, tables, pw, g_pre1, g_post1, ssd_norm_w, sinks, g_pre2, g_post2):
    batch, seq, _ = x.shape
    x2 = x.reshape(batch * seq, D_MODEL)
    proj2, dt2 = _in_projection(x2, mod4, g_pre1, pw["w_main"], pw["w_dt"], tables, batch, seq)
    proj3 = proj2.reshape(batch, seq, PROJ_W)
    dt3 = dt2.reshape(batch, seq, DT_W)
    xact3 = _conv_silu(proj3, pw["conv_taps"])
    attn, yf, yb = _token_mixers(proj3, xact3, dt3, sinks, pw["dt_bias"], pw["a_neg"], pw["e_f"], pw["e_b"],
                                 pw["dskip_e"])
    x1 = _out_projection(x2, mod4, yf.reshape(batch * seq, D_INNER), yb.reshape(batch * seq, D_INNER), proj2,
                         attn.reshape(batch * seq, D_MODEL), ssd_norm_w, pw["w_out"], g_post1, batch, seq)
    out = _ffn(x1, mod4, g_pre2, pw["w_gu"], pw["w_d"], g_post2, batch, seq)
    return out.reshape(batch, seq, D_MODEL)


def kernel(x_prompt, x_sample, c_prompt, c_sample, w_ada, b_ada, g_pre1, g_post1, w_in, conv_w, conv_b, a_log_f,
           a_log_b, dt_bias_f, dt_bias_b, d_skip, ssd_norm_w, sinks, w_out, g_pre2, g_post2, w_gu, w_down):
    assert w_ada.shape[0] == 1, "single-layer problem"
    nb_p, nb_s = c_prompt.shape[0], c_sample.shape[0]
    nb = -(-(nb_p + nb_s) // 8) * 8
    c_all = jnp.pad(jnp.concatenate([c_prompt, c_sample], axis=0), ((0, nb - nb_p - nb_s), (0, 0)))
    mod = _modulation(c_all, w_ada[0], b_ada[0]).reshape(nb, 6, 1, D_MODEL)
    pw = _prep_weights(w_in[0], conv_w[0], conv_b[0], a_log_f[0], a_log_b[0], dt_bias_f[0], dt_bias_b[0],
                       d_skip[0], w_out[0], w_gu[0], w_down[0])
    outs = []
    for x, mod4 in ((x_prompt, mod[:nb_p]), (x_sample, mod[nb_p:nb_p + nb_s])):
        tables = _rope_tables(x.shape[1])
        outs.append(_encoder_layer(x, mod4, tables, pw, g_pre1[0], g_post1[0], ssd_norm_w[0], sinks[0],
                                   g_pre2[0], g_post2[0]))
    return tuple(outs)
```

```python
import functools
import math

import jax
import jax.numpy as jnp
from jax import lax
from jax.experimental import pallas as pl
from jax.experimental.pallas import tpu as pltpu

F32 = jnp.float32
BF16 = jnp.bfloat16

D_MODEL = 2048
N_HEADS = 32
N_KV_HEADS = 8
HEAD_DIM = 64
KV_WIDTH = N_KV_HEADS * HEAD_DIM
BLOCK = 128
ROPE_THETA = 500000.0
ROPE_DIM = HEAD_DIM // 4
SSD_HEADS = 32
SSD_HEAD_DIM = 64
D_INNER = SSD_HEADS * SSD_HEAD_DIM
D_STATE = 128
N_GROUPS = 8
CONV_W = 5
CONV_DIM = D_INNER + 2 * N_GROUPS * D_STATE
CHUNK = 128
D_FF = 5632
EPS = 1e-6
LOG2E = math.log2(math.e)

LANES = 128
HALO = 16

COL_XBC = 0
COL_GATE = COL_XBC + CONV_DIM
COL_Q = COL_GATE + 2 * D_MODEL
COL_Z = COL_Q + D_MODEL
COL_K = COL_Z + D_INNER
COL_V = COL_K + KV_WIDTH
PROJ_W = COL_V + KV_WIDTH
DT_W = LANES

VMEM_LIMIT = 56 * 1024 * 1024


def _cparams(sem):
    return pltpu.CompilerParams(dimension_semantics=sem, vmem_limit_bytes=VMEM_LIMIT)


def _silu(v):
    return v * jax.nn.sigmoid(v)


NORM_ROWS = 16


def _rowwise_rms_apply(src_ref, apply):
    n_rows, width = src_ref.shape
    v = src_ref[...]
    inv = lax.rsqrt(jnp.sum(v * v, axis=-1, keepdims=True) * (1.0 / width) + EPS)
    for r in range(n_rows // NORM_ROWS):
        rows = slice(r * NORM_ROWS, (r + 1) * NORM_ROWS)
        apply(rows, src_ref[rows, :], inv[rows])


def _modulated_prenorm(x_ref, g_ref, sc_ref, sh_ref, h_ref):
    tile = (NORM_ROWS, x_ref.shape[1])
    gmod = jnp.broadcast_to(g_ref[...] * (1.0 + sc_ref[...]), tile)
    shift = jnp.broadcast_to(sh_ref[...], tile)

    def apply(rows, x, inv):
        h_ref[rows, :] = (x * inv * gmod + shift).astype(BF16)

    _rowwise_rms_apply(x_ref, apply)


def _gated_postnorm_residual(f_ref, x_ref, gpost_ref, gt_ref, o_ref):
    gg = jnp.broadcast_to(gpost_ref[...] * gt_ref[...], (NORM_ROWS, f_ref.shape[1]))

    def apply(rows, f, inv):
        o_ref[rows, :] = x_ref[rows, :] + f * inv * gg

    _rowwise_rms_apply(f_ref, apply)


def _mod_kernel(c_ref, w_ref, b_ref, o_ref):
    c = c_ref[...]
    s = _silu(c).astype(BF16)
    o_ref[...] = jnp.dot(s, w_ref[...].astype(BF16), preferred_element_type=F32) + b_ref[...]


def _modulation(c_all, w_ada, b_ada):
    nb = c_all.shape[0]
    n = w_ada.shape[1]
    tn = 1024
    return pl.pallas_call(
        _mod_kernel,
        grid=(n // tn,),
        in_specs=[pl.BlockSpec((nb, D_MODEL), lambda j: (0, 0)),
                  pl.BlockSpec((D_MODEL, tn), lambda j: (0, j)),
                  pl.BlockSpec((1, tn), lambda j: (0, j))],
        out_specs=pl.BlockSpec((nb, tn), lambda j: (0, j)),
        out_shape=jax.ShapeDtypeStruct((nb, n), F32),
        compiler_params=_cparams(("arbitrary",)),
        name="adaln_mod",
    )(c_all, w_ada, b_ada.reshape(1, n))


def _rope_table_kernel(cos_ref, sin_lo_ref, sin_hi_ref):
    tm = cos_ref.shape[0]
    half = ROPE_DIM // 2
    pos = (pl.program_id(0) * tm + lax.broadcasted_iota(jnp.int32, (tm, LANES), 0)).astype(F32)
    d = lax.broadcasted_iota(jnp.int32, (tm, LANES), 1) % HEAD_DIM
    fidx = (d % half).astype(F32)
    inv = jnp.exp(fidx * (-2.0 / ROPE_DIM * math.log(ROPE_THETA)))
    ang = pos * inv
    cos = jnp.cos(ang)
    sin = jnp.sin(ang)
    cos_ref[...] = jnp.where(d < ROPE_DIM, cos, 1.0)
    sin_lo_ref[...] = jnp.where(d < half, -sin, 0.0)
    sin_hi_ref[...] = jnp.where((d >= half) & (d < ROPE_DIM), sin, 0.0)


def _rope_tables(seq):
    tm = min(seq, 2048)
    shp = jax.ShapeDtypeStruct((seq, LANES), F32)
    spec = pl.BlockSpec((tm, LANES), lambda i: (i, 0))
    return pl.pallas_call(
        _rope_table_kernel,
        grid=(seq // tm,),
        in_specs=[],
        out_specs=[spec, spec, spec],
        out_shape=[shp, shp, shp],
        compiler_params=_cparams(("arbitrary",)),
        name="rope_tables",
    )()


def _rope_slab(t, cos, sin_lo, sin_hi):
    half = ROPE_DIM // 2
    return t * cos + pltpu.roll(t, LANES - half, 1) * sin_lo + pltpu.roll(t, half, 1) * sin_hi


def _inproj_kernel(x_ref, sc_ref, sh_ref, g_ref, w_ref, wdt_ref, cos_ref, slo_ref, shi_ref,
                   proj_ref, dt_ref, h_scr, *, tn):
    j = pl.program_id(1)

    @pl.when(j == 0)
    def _():
        _modulated_prenorm(x_ref, g_ref, sc_ref, sh_ref, h_scr)
        dt_ref[...] = jnp.dot(h_scr[...], wdt_ref[...], preferred_element_type=F32)

    q_lo, q_hi = COL_Q // tn, (COL_Q + D_MODEL) // tn
    k_tile = COL_K // tn
    k_slabs = KV_WIDTH // LANES
    is_q = (j >= q_lo) & (j < q_hi)
    is_k = j == k_tile
    tm = h_scr.shape[0]
    n_parts = 2
    rows = tm // n_parts

    def project(n_rope_slabs, scale):
        for p in range(n_parts):
            rs = slice(p * rows, (p + 1) * rows)
            acc = jnp.dot(h_scr[rs, :], w_ref[...], preferred_element_type=F32)
            if n_rope_slabs == 0:
                proj_ref[rs, :] = acc.astype(BF16)
                continue
            cos, slo, shi = cos_ref[rs, :], slo_ref[rs, :], shi_ref[rs, :]
            for s in range(tn // LANES):
                t = acc[:, s * LANES:(s + 1) * LANES]
                if s < n_rope_slabs:
                    t = _rope_slab(t, cos, slo, shi)
                    if scale != 1.0:
                        t = t * scale
                proj_ref[rs, s * LANES:(s + 1) * LANES] = t.astype(BF16)

    @pl.when(is_q)
    def _():
        project(tn // LANES, HEAD_DIM ** -0.5 * LOG2E)

    @pl.when(is_k)
    def _():
        project(k_slabs, 1.0)

    @pl.when(jnp.logical_not(is_q | is_k))
    def _():
        project(0, 1.0)


def _in_projection(x2, mod4, g_pre, w_main, w_dt, tables, batch, seq):
    t_rows = batch * seq
    tm = min(seq, 1024)
    tn = 1024
    assert COL_K % tn == 0 and COL_Q % tn == 0 and D_MODEL % tn == 0 and PROJ_W % tn == 0
    per_b = seq // tm
    cos, slo, shi = tables
    tab_spec = pl.BlockSpec((tm, LANES), lambda i, j: (i % per_b, 0))
    mod_spec = lambda k: pl.BlockSpec((None, None, 1, D_MODEL), lambda i, j: (i // per_b, k, 0, 0))
    return pl.pallas_call(
        functools.partial(_inproj_kernel, tn=tn),
        grid=(t_rows // tm, PROJ_W // tn),
        in_specs=[pl.BlockSpec((tm, D_MODEL), lambda i, j: (i, 0)),
                  mod_spec(1), mod_spec(0),
                  pl.BlockSpec((1, D_MODEL), lambda i, j: (0, 0)),
                  pl.BlockSpec((D_MODEL, tn), lambda i, j: (0, j)),
                  pl.BlockSpec((D_MODEL, DT_W), lambda i, j: (0, 0)),
                  tab_spec, tab_spec, tab_spec],
        out_specs=[pl.BlockSpec((tm, tn), lambda i, j: (i, j)),
                   pl.BlockSpec((tm, DT_W), lambda i, j: (i, 0))],
        out_shape=[jax.ShapeDtypeStruct((t_rows, PROJ_W), BF16),
                   jax.ShapeDtypeStruct((t_rows, DT_W), F32)],
        scratch_shapes=[pltpu.VMEM((tm, D_MODEL), BF16)],
        compiler_params=_cparams(("arbitrary", "arbitrary")),
        name="norm_in_proj",
    )(x2, mod4, mod4, g_pre.reshape(1, D_MODEL), w_main, w_dt, cos, slo, shi)


def _attn_stages(sink_ref, q_ref, kp_ref, kc_ref, kn_ref, vp_ref, vc_ref, vn_ref, o_ref):
    i = pl.program_id(1)
    nblk = pl.num_programs(1)
    rows2 = 2 * BLOCK
    r = lax.broadcasted_iota(jnp.int32, (rows2, BLOCK), 0) % BLOCK
    c = lax.broadcasted_iota(jnp.int32, (rows2, BLOCK), 1)
    top = lax.broadcasted_iota(jnp.int32, (rows2, 1), 0) < BLOCK
    neg = jnp.float32(-jnp.inf)
    bias_p = jnp.where((c >= r) & (i > 0), 0.0, neg)
    bias_n = jnp.where((c <= r) & (i < nblk - 1), 0.0, neg)
    lane3 = lax.broadcasted_iota(jnp.int32, (3 * BLOCK, LANES), 1)
    low3 = lane3 < HEAD_DIM
    low2 = c < HEAD_DIM

    def padded_variants(p_ref, c_ref, n_ref, slab):
        sl = slice(slab * LANES, (slab + 1) * LANES)
        x = jnp.concatenate([p_ref[:, sl], c_ref[:, sl], n_ref[:, sl]], axis=0)
        xr = pltpu.bitcast(pltpu.roll(pltpu.bitcast(x, jnp.uint32), HEAD_DIM, 1), BF16)
        zero = jnp.zeros_like(x)
        return ((jnp.where(low3, x, zero), jnp.where(low3, zero, xr)),
                (jnp.where(low3, xr, zero), jnp.where(low3, zero, x)))

    variants = {}

    def scores(g):
        slab, par = divmod(g, 2)
        if slab not in variants:
            variants[slab] = (padded_variants(kp_ref, kc_ref, kn_ref, slab),
                              padded_variants(vp_ref, vc_ref, vn_ref, slab))
        k_var, v_var = variants[slab]
        qq = jnp.concatenate([q_ref[:, (2 * g) * LANES:(2 * g + 1) * LANES],
                              q_ref[:, (2 * g + 1) * LANES:(2 * g + 2) * LANES]], axis=0)
        kk = jnp.concatenate(k_var[par], axis=0)
        s = lax.dot_general(qq, kk, (((1,), (1,)), ((), ())), preferred_element_type=F32)
        return s, jnp.concatenate(v_var[par], axis=0)

    def finish(g, s, vv):
        probs, inv = [], []
        for e in range(2):
            base = e * 3 * BLOCK
            sp = s[:, base:base + BLOCK] + bias_p
            sc = s[:, base + BLOCK:base + 2 * BLOCK]
            sn = s[:, base + 2 * BLOCK:base + 3 * BLOCK] + bias_n
            sink = jnp.where(top, sink_ref[4 * g + e], sink_ref[4 * g + 2 + e]) * LOG2E
            m = jnp.maximum(jnp.max(jnp.maximum(jnp.maximum(sp, sc), sn), axis=-1, keepdims=True), sink)
            pp, pc, pn = jnp.exp2(sp - m), jnp.exp2(sc - m), jnp.exp2(sn - m)
            denom = jnp.sum(pp + pc + pn, axis=-1, keepdims=True) + jnp.exp2(sink - m)
            probs += [pp.astype(BF16), pc.astype(BF16), pn.astype(BF16)]
            inv.append(1.0 / denom)
        o = jnp.dot(jnp.concatenate(probs, axis=1), vv, preferred_element_type=F32)
        o = o * jnp.where(low2, inv[0], inv[1])
        o_ref[:, (2 * g) * LANES:(2 * g + 1) * LANES] = o[:BLOCK].astype(BF16)
        o_ref[:, (2 * g + 1) * LANES:(2 * g + 2) * LANES] = o[BLOCK:].astype(BF16)

    return scores, finish


CONV_ROWS = 512
N_SLABS = CONV_DIM // LANES


def _conv_kernel(xp_ref, xm_ref, xn_ref, cw_ref, o_ref, in_scr, out_scr):
    i = pl.program_id(1)
    n = pl.num_programs(1)
    has_prev = (i > 0).astype(F32)
    has_next = (i < n - 1).astype(F32)
    rows = CONV_ROWS
    for cs in range(N_SLABS):
        sl = slice(cs * LANES, (cs + 1) * LANES)
        in_scr[cs, 0:HALO, :] = xp_ref[:, sl].astype(F32) * has_prev
        in_scr[cs, HALO:HALO + rows, :] = xm_ref[:, sl].astype(F32)
        in_scr[cs, HALO + rows:, :] = xn_ref[:, sl].astype(F32) * has_next

    def slab_body(cs, carry):
        cw = cw_ref[cs]
        taps = [jnp.broadcast_to(cw[w:w + 1, :], (8, LANES)) for w in range(CONV_W)]
        bias = jnp.broadcast_to(cw[CONV_W:CONV_W + 1, :], (8, LANES))
        for g in range(rows // 16):
            for e in range(2):
                acc = bias
                for w in range(CONV_W):
                    start = HALO + 16 * g + e + w - CONV_W // 2
                    acc = acc + in_scr[cs, pl.ds(start, 8, stride=2), :] * taps[w]
                half = 0.5 * acc
                out_scr[cs, pl.ds(16 * g + e, 8, stride=2), :] = half + half * jnp.tanh(half)
        return carry

    lax.fori_loop(0, N_SLABS, slab_body, 0)
    for cs in range(N_SLABS):
        o_ref[:, cs * LANES:(cs + 1) * LANES] = out_scr[cs].astype(BF16)


def _conv_silu(proj3, conv_taps):
    batch, seq, _ = proj3.shape
    rows = CONV_ROWS
    hpr = rows // HALO
    nh = seq // HALO
    col = COL_XBC // CONV_DIM
    return pl.pallas_call(
        _conv_kernel,
        grid=(batch, seq // rows),
        in_specs=[pl.BlockSpec((None, HALO, CONV_DIM), lambda b, i: (b, jnp.maximum(i * hpr - 1, 0), col)),
                  pl.BlockSpec((None, rows, CONV_DIM), lambda b, i: (b, i, col)),
                  pl.BlockSpec((None, HALO, CONV_DIM), lambda b, i: (b, jnp.minimum((i + 1) * hpr, nh - 1), col)),
                  pl.BlockSpec((N_SLABS, 8, LANES), lambda b, i: (0, 0, 0))],
        out_specs=pl.BlockSpec((None, rows, CONV_DIM), lambda b, i: (b, i, 0)),
        out_shape=jax.ShapeDtypeStruct((batch, seq, CONV_DIM), BF16),
        scratch_shapes=[pltpu.VMEM((N_SLABS, rows + 2 * HALO, LANES), F32),
                        pltpu.VMEM((N_SLABS, rows, LANES), F32)],
        compiler_params=_cparams(("arbitrary", "arbitrary")),
        name="conv_silu",
    )(proj3, proj3, proj3, conv_taps)


def _ssd_direction(xa_ref, dt_ref, bias_ref, a_ref, e_ref, dskip_ref, st_ref, out_ref, *, reverse):
    lane0 = SSD_HEADS if reverse else 0
    xs = xa_ref[:, :D_INNER]
    bm = xa_ref[:, D_INNER:D_INNER + N_GROUPS * D_STATE]
    cm = xa_ref[:, D_INNER + N_GROUPS * D_STATE:]

    dt = jax.nn.softplus(dt_ref[...] + bias_ref[...])
    da = dt * a_ref[...]
    r = lax.broadcasted_iota(jnp.int32, (CHUNK, CHUNK), 0)
    c = lax.broadcasted_iota(jnp.int32, (CHUNK, CHUNK), 1)
    causal = (c >= r) if reverse else (c <= r)
    tri = causal.astype(BF16)
    d1 = da.astype(BF16)
    r1 = da - d1.astype(F32)
    d2 = r1.astype(BF16)
    d3 = (r1 - d2.astype(F32)).astype(BF16)
    cum = (jnp.dot(tri, d1, preferred_element_type=F32) + jnp.dot(tri, d2, preferred_element_type=F32)
           + jnp.dot(tri, d3, preferred_element_type=F32)) * LOG2E
    src_t = (cum - jnp.log2(dt)).T
    edge = CHUNK - 1 if not reverse else 0
    total = cum[edge:edge + 1, :]
    exp_cum = jnp.exp2(cum)
    decay = jnp.exp2(total - cum)

    e = e_ref[...]

    def expand(v):
        return jnp.dot(v.astype(BF16), e, preferred_element_type=F32)

    x_dec = xs * expand(dt * decay).astype(BF16)
    exp_cum_e = expand(exp_cum)
    tot8 = jnp.broadcast_to(jnp.exp2(total), (8, DT_W))
    tot_hi = tot8.astype(BF16)
    tot_lo = (tot8 - tot_hi.astype(F32)).astype(BF16)
    chunk_decay = (jnp.dot(tot_hi, e, preferred_element_type=F32)
                   + jnp.dot(tot_lo, e, preferred_element_type=F32))[0:1, :]

    heads_per_group = SSD_HEADS // N_GROUPS
    gw = heads_per_group * SSD_HEAD_DIM
    lane = lax.broadcasted_iota(jnp.int32, (CHUNK, LANES), 1)
    neg = jnp.float32(-jnp.inf)

    def group_matmuls(g):
        b_g = bm[:, g * D_STATE:(g + 1) * D_STATE]
        c_g = cm[:, g * D_STATE:(g + 1) * D_STATE]
        cb = lax.dot_general(c_g, b_g, (((1,), (1,)), ((), ())), preferred_element_type=F32)
        gs = slice(g * gw, (g + 1) * gw)
        st_prev = st_ref[:, gs]
        y_off = jnp.dot(c_g, st_prev.astype(BF16), preferred_element_type=F32)
        upd = lax.dot_general(b_g, x_dec[:, gs], (((0,), (0,)), ((), ())), preferred_element_type=F32)
        st_ref[:, gs] = st_prev * chunk_decay[:, gs] + upd
        return cb, y_off

    def group_finish(g, mm):
        cb, y_off = mm
        gs = slice(g * gw, (g + 1) * gw)
        y = y_off * exp_cum_e[:, gs]
        y_diag = []
        for pair in range(heads_per_group // 2):
            ms = []
            for hh in range(2):
                k = lane0 + g * heads_per_group + pair * 2 + hh
                seg = cum[:, k:k + 1] - src_t[k:k + 1, :]
                ms.append((cb * jnp.exp2(jnp.where(causal, seg, neg))).astype(BF16))
            col = g * gw + pair * LANES
            xp = xa_ref[:, col:col + LANES]
            zero = jnp.zeros_like(xp)
            rhs = jnp.concatenate([jnp.where(lane < SSD_HEAD_DIM, xp, zero),
                                   jnp.where(lane >= SSD_HEAD_DIM, xp, zero)], axis=0)
            y_diag.append(jnp.dot(jnp.concatenate(ms, axis=1), rhs, preferred_element_type=F32))
        y = y + jnp.concatenate(y_diag, axis=1)
        if dskip_ref is not None:
            y = y + dskip_ref[:, gs] * xs[:, gs].astype(F32)
        out_ref[:, gs] = y.astype(out_ref.dtype)

    return group_matmuls, group_finish


def _mixers_kernel(sink_ref, q_ref, kp_ref, kc_ref, kn_ref, vp_ref, vc_ref, vn_ref,
                   xa_f, dt_f, xa_b, dt_b, bias_ref, a_ref, e_f, e_b, dskip_ref,
                   attn_ref, yf_ref, yb_ref, st_f, st_b):
    @pl.when(pl.program_id(1) == 0)
    def _():
        st_f[...] = jnp.zeros_like(st_f)
        st_b[...] = jnp.zeros_like(st_b)

    scores, attn_finish = _attn_stages(sink_ref, q_ref, kp_ref, kc_ref, kn_ref, vp_ref, vc_ref, vn_ref, attn_ref)
    dirs = [_ssd_direction(xa_f, dt_f, bias_ref, a_ref, e_f, dskip_ref, st_f, yf_ref, reverse=False),
            _ssd_direction(xa_b, dt_b, bias_ref, a_ref, e_b, None, st_b, yb_ref, reverse=True)]
    assert N_GROUPS == N_KV_HEADS
    pend_a = scores(0)
    pend_s = [mm(0) for mm, _ in dirs]
    for g in range(N_GROUPS):
        last = g + 1 == N_GROUPS
        next_a = None if last else scores(g + 1)
        next_s = None if last else [mm(g + 1) for mm, _ in dirs]
        attn_finish(g, *pend_a)
        for (_, fin), mm_out in zip(dirs, pend_s):
            fin(g, mm_out)
        pend_a, pend_s = next_a, next_s


def _token_mixers(proj3, xact3, dt3, sinks, dt_bias, a_neg, e_f, e_b, dskip_e):
    batch, seq, _ = proj3.shape
    n = seq // BLOCK
    assert BLOCK == CHUNK
    kcol, vcol = COL_K // KV_WIDTH, COL_V // KV_WIDTH
    prev = lambda col: pl.BlockSpec((None, BLOCK, KV_WIDTH), lambda b, i: (b, jnp.maximum(i - 1, 0), col))
    cur = lambda col: pl.BlockSpec((None, BLOCK, KV_WIDTH), lambda b, i: (b, i, col))
    nxt = lambda col: pl.BlockSpec((None, BLOCK, KV_WIDTH), lambda b, i: (b, jnp.minimum(i + 1, n - 1), col))

    def ssd_specs(chunk_of):
        return [pl.BlockSpec((None, CHUNK, CONV_DIM), lambda b, i: (b, chunk_of(i), 0)),
                pl.BlockSpec((None, CHUNK, DT_W), lambda b, i: (b, chunk_of(i), 0))]

    fwd = lambda i: i
    bwd = lambda i: n - 1 - i
    const = lambda shape: pl.BlockSpec(shape, lambda b, i: (0,) * len(shape))
    row_f = pl.BlockSpec((None, BLOCK, D_MODEL), lambda b, i: (b, i, 0))
    row_b = pl.BlockSpec((None, BLOCK, D_MODEL), lambda b, i: (b, n - 1 - i, 0))
    shape = jax.ShapeDtypeStruct((batch, seq, D_MODEL), BF16)
    return pl.pallas_call(
        _mixers_kernel,
        grid=(batch, n),
        in_specs=[pl.BlockSpec(memory_space=pltpu.SMEM),
                  pl.BlockSpec((None, BLOCK, D_MODEL), lambda b, i: (b, i, COL_Q // D_MODEL)),
                  prev(kcol), cur(kcol), nxt(kcol), prev(vcol), cur(vcol), nxt(vcol)]
                 + ssd_specs(fwd) + ssd_specs(bwd)
                 + [const((1, DT_W)), const((1, DT_W)), const((DT_W, D_INNER)), const((DT_W, D_INNER)),
                    const((1, D_INNER))],
        out_specs=[row_f, row_f, row_b],
        out_shape=[shape, shape, shape],
        scratch_shapes=[pltpu.VMEM((D_STATE, D_INNER), F32), pltpu.VMEM((D_STATE, D_INNER), F32)],
        compiler_params=_cparams(("arbitrary", "arbitrary")),
        name="token_mixers",
    )(sinks, proj3, proj3, proj3, proj3, proj3, proj3, proj3, xact3, dt3, xact3, dt3,
      dt_bias, a_neg, e_f, e_b, dskip_e)


def _outproj_kernel(x_ref, gt_ref, yf_ref, yb_ref, z_ref, attn_ref, ga_ref, gs_ref, nw_ref, w_ref, gpost_ref,
                    o_ref):
    z = z_ref[...]
    y = (yf_ref[...].astype(F32) + yb_ref[...].astype(F32)) * (z * jax.nn.sigmoid(z)).astype(F32)
    gw = D_INNER // N_GROUPS
    parts = []
    for g in range(N_GROUPS):
        yg = y[:, g * gw:(g + 1) * gw]
        parts.append(yg * lax.rsqrt(jnp.mean(yg * yg, axis=-1, keepdims=True) + EPS))
    ssd = (jnp.concatenate(parts, axis=1) * nw_ref[...]).astype(BF16)
    merged = jax.nn.sigmoid(ga_ref[...]) * attn_ref[...] + jax.nn.sigmoid(gs_ref[...]) * ssd
    mix = jnp.dot(merged, w_ref[...], preferred_element_type=F32)
    nrm = mix * lax.rsqrt(jnp.mean(mix * mix, axis=-1, keepdims=True) + EPS) * gpost_ref[...]
    o_ref[...] = x_ref[...] + gt_ref[...] * nrm


def _out_projection(x2, mod4, yf2, yb2, proj2, attn2, ssd_norm_w, w_out, g_post, batch, seq):
    t_rows = batch * seq
    tm = min(seq, 256)
    per_b = seq // tm
    row = lambda col: pl.BlockSpec((tm, D_MODEL), lambda i: (i, col))
    vec = pl.BlockSpec((1, D_MODEL), lambda i: (0, 0))
    return pl.pallas_call(
        _outproj_kernel,
        grid=(t_rows // tm,),
        in_specs=[row(0),
                  pl.BlockSpec((None, None, 1, D_MODEL), lambda i: (i // per_b, 2, 0, 0)),
                  row(0), row(0), row(COL_Z // D_MODEL), row(0),
                  row(COL_GATE // D_MODEL), row(COL_GATE // D_MODEL + 1),
                  vec,
                  pl.BlockSpec((D_MODEL, D_MODEL), lambda i: (0, 0)),
                  vec],
        out_specs=row(0),
        out_shape=jax.ShapeDtypeStruct((t_rows, D_MODEL), F32),
        compiler_params=_cparams(("arbitrary",)),
        name="merge_out_proj",
    )(x2, mod4, yf2, yb2, proj2, attn2, proj2, proj2, ssd_norm_w.reshape(1, D_INNER), w_out,
      g_post.reshape(1, D_MODEL))


FFN_PART_ROWS = 512


def _ffn_kernel(x_ref, sc_ref, sh_ref, gt_ref, gpre_ref, wg_ref, wu_ref, wd_ref, gpost_ref, o_ref, h_scr):
    j = pl.program_id(1)

    @pl.when(j == 0)
    def _():
        _modulated_prenorm(x_ref, gpre_ref, sc_ref, sh_ref, h_scr)
        o_ref[...] = jnp.zeros_like(o_ref)

    for p in range(h_scr.shape[0] // FFN_PART_ROWS):
        rows = slice(p * FFN_PART_ROWS, (p + 1) * FFN_PART_ROWS)
        h = h_scr[rows, :]
        gate = jnp.dot(h, wg_ref[...], preferred_element_type=F32)
        up = jnp.dot(h, wu_ref[...], preferred_element_type=F32)
        act = (_silu(gate) * up).astype(BF16)
        o_ref[rows, :] += jnp.dot(act, wd_ref[...], preferred_element_type=F32)

    @pl.when(j == pl.num_programs(1) - 1)
    def _():
        _gated_postnorm_residual(o_ref, x_ref, gpost_ref, gt_ref, o_ref)


def _ffn(x2, mod4, g_pre, w_gu, w_d, g_post, batch, seq):
    t_rows = batch * seq
    tm = min(seq, 1024)
    tf = 512
    assert tm % FFN_PART_ROWS == 0
    up0 = D_FF // tf
    per_b = seq // tm
    mod_spec = lambda k: pl.BlockSpec((None, None, 1, D_MODEL), lambda i, j: (i // per_b, k, 0, 0))
    vec = pl.BlockSpec((1, D_MODEL), lambda i, j: (0, 0))
    return pl.pallas_call(
        _ffn_kernel,
        grid=(t_rows // tm, D_FF // tf),
        in_specs=[pl.BlockSpec((tm, D_MODEL), lambda i, j: (i, 0)),
                  mod_spec(4), mod_spec(3), mod_spec(5), vec,
                  pl.BlockSpec((D_MODEL, tf), lambda i, j: (0, j)),
                  pl.BlockSpec((D_MODEL, tf), lambda i, j: (0, up0 + j)),
                  pl.BlockSpec((tf, D_MODEL), lambda i, j: (j, 0)),
                  vec],
        out_specs=pl.BlockSpec((tm, D_MODEL), lambda i, j: (i, 0)),
        out_shape=jax.ShapeDtypeStruct((t_rows, D_MODEL), F32),
        scratch_shapes=[pltpu.VMEM((tm, D_MODEL), BF16)],
        compiler_params=_cparams(("arbitrary", "arbitrary")),
        name="swiglu_ffn",
    )(x2, mod4, mod4, mod4, g_pre.reshape(1, D_MODEL), w_gu, w_gu, w_d, g_post.reshape(1, D_MODEL))


def _prep_weights(w_in, conv_w, conv_b, a_log_f, a_log_b, dt_bias_f, dt_bias_b, d_skip, w_out, w_gu, w_down):
    o_q, o_k, o_v = 0, D_MODEL, D_MODEL + KV_WIDTH
    o_z = o_v + KV_WIDTH
    o_xbc = o_z + D_INNER
    o_dt = o_xbc + CONV_DIM
    o_gate = o_dt + 2 * SSD_HEADS
    seg = lambda off, width: w_in[:, off:off + width]
    w_main = jnp.concatenate([seg(o_xbc, CONV_DIM), seg(o_gate, 2 * D_MODEL), seg(o_q, D_MODEL),
                              seg(o_z, D_INNER), seg(o_k, KV_WIDTH), seg(o_v, KV_WIDTH)], axis=1).astype(BF16)
    w_dt = jnp.pad(seg(o_dt, 2 * SSD_HEADS), ((0, 0), (0, DT_W - 2 * SSD_HEADS))).astype(BF16)
    pad_l = lambda v: jnp.pad(v, (0, DT_W - v.shape[0])).reshape(1, DT_W)
    dt_bias = pad_l(jnp.concatenate([dt_bias_f, dt_bias_b]))
    a_neg = pad_l(-jnp.exp(jnp.concatenate([a_log_f, a_log_b])))
    taps8 = jnp.concatenate([conv_w, conv_b[None, :], jnp.zeros((8 - CONV_W - 1, CONV_DIM), F32)], axis=0)
    conv_taps = taps8.reshape(8, N_SLABS, LANES).transpose(1, 0, 2)
    head_of_lane = jnp.arange(D_INNER) // SSD_HEAD_DIM
    rows = jnp.arange(DT_W)[:, None]
    e_f = (rows == head_of_lane[None, :]).astype(BF16)
    e_b = (rows == head_of_lane[None, :] + SSD_HEADS).astype(BF16)
    dskip_e = jnp.repeat(d_skip, SSD_HEAD_DIM).reshape(1, D_INNER)
    return dict(w_main=w_main, w_dt=w_dt, dt_bias=dt_bias, a_neg=a_neg, conv_taps=conv_taps,
                e_f=e_f, e_b=e_b, dskip_e=dskip_e,
                w_out=w_out.astype(BF16), w_gu=w_gu.astype(BF16),
                w_d=w_down.astype(BF16))


def _encoder_layer(x, mod4, tables, pw, g_pre1, g_post1, ssd_norm_w, sinks, g_pre2, g_post2):
    batch, seq, _ = x.shape
    x2 = x.reshape(batch * seq, D_MODEL)
    proj2, dt2 = _in_projection(x2, mod4, g_pre1, pw["w_main"], pw["w_dt"], tables, batch, seq)
    proj3 = proj2.reshape(batch, seq, PROJ_W)
    dt3 = dt2.reshape(batch, seq, DT_W)
    xact3 = _conv_silu(proj3, pw["conv_taps"])
    attn, yf, yb = _token_mixers(proj3, xact3, dt3, sinks, pw["dt_bias"], pw["a_neg"], pw["e_f"], pw["e_b"],
                                 pw["dskip_e"])
    x1 = _out_projection(x2, mod4, yf.reshape(batch * seq, D_INNER), yb.reshape(batch * seq, D_INNER), proj2,
                         attn.reshape(batch * seq, D_MODEL), ssd_norm_w, pw["w_out"], g_post1, batch, seq)
    out = _ffn(x1, mod4, g_pre2, pw["w_gu"], pw["w_d"], g_post2, batch, seq)
    return out.reshape(batch, seq, D_MODEL)


def kernel(x_prompt, x_sample, c_prompt, c_sample, w_ada, b_ada, g_pre1, g_post1, w_in, conv_w, conv_b, a_log_f,
           a_log_b, dt_bias_f, dt_bias_b, d_skip, ssd_norm_w, sinks, w_out, g_pre2, g_post2, w_gu, w_down):
    assert w_ada.shape[0] == 1, "single-layer problem"
    nb_p, nb_s = c_prompt.shape[0], c_sample.shape[0]
    nb = -(-(nb_p + nb_s) // 8) * 8
    c_all = jnp.pad(jnp.concatenate([c_prompt, c_sample], axis=0), ((0, nb - nb_p - nb_s), (0, 0)))
    mod = _modulation(c_all, w_ada[0], b_ada[0]).reshape(nb, 6, 1, D_MODEL)
    pw = _prep_weights(w_in[0], conv_w[0], conv_b[0], a_log_f[0], a_log_b[0], dt_bias_f[0], dt_bias_b[0],
                       d_skip[0], w_out[0], w_gu[0], w_down[0])
    outs = []
    for x, mod4 in ((x_prompt, mod[:nb_p]), (x_sample, mod[nb_p:nb_p + nb_s])):
        tables = _rope_tables(x.shape[1])
        outs.append(_encoder_layer(x, mod4, tables, pw, g_pre1[0], g_post1[0], ssd_norm_w[0], sinks[0],
                                   g_pre2[0], g_post2[0]))
    return tuple(outs)
```

```python
import functools
import math

import jax
import jax.numpy as jnp
from jax import lax
from jax.experimental import pallas as pl
from jax.experimental.pallas import tpu as pltpu

F32 = jnp.float32
BF16 = jnp.bfloat16

D_MODEL = 2048
N_HEADS = 32
N_KV_HEADS = 8
HEAD_DIM = 64
KV_WIDTH = N_KV_HEADS * HEAD_DIM
BLOCK = 128
ROPE_THETA = 500000.0
ROPE_DIM = HEAD_DIM // 4
SSD_HEADS = 32
SSD_HEAD_DIM = 64
D_INNER = SSD_HEADS * SSD_HEAD_DIM
D_STATE = 128
N_GROUPS = 8
CONV_W = 5
CONV_DIM = D_INNER + 2 * N_GROUPS * D_STATE
CHUNK = 128
D_FF = 5632
EPS = 1e-6
LOG2E = math.log2(math.e)

LANES = 128
HALO = 16

COL_XBC = 0
COL_GATE = COL_XBC + CONV_DIM
COL_Q = COL_GATE + 2 * D_MODEL
COL_Z = COL_Q + D_MODEL
COL_K = COL_Z + D_INNER
COL_V = COL_K + KV_WIDTH
PROJ_W = COL_V + KV_WIDTH
DT_W = LANES

VMEM_LIMIT = 56 * 1024 * 1024


def _cparams(sem):
    return pltpu.CompilerParams(dimension_semantics=sem, vmem_limit_bytes=VMEM_LIMIT)


def _silu(v):
    return v * jax.nn.sigmoid(v)


NORM_ROWS = 16


def _rowwise_rms_apply(src_ref, apply):
    n_rows, width = src_ref.shape
    v = src_ref[...]
    inv = lax.rsqrt(jnp.sum(v * v, axis=-1, keepdims=True) * (1.0 / width) + EPS)
    for r in range(n_rows // NORM_ROWS):
        rows = slice(r * NORM_ROWS, (r + 1) * NORM_ROWS)
        apply(rows, src_ref[rows, :], inv[rows])


def _modulated_prenorm(x_ref, g_ref, sc_ref, sh_ref, h_ref):
    tile = (NORM_ROWS, x_ref.shape[1])
    gmod = jnp.broadcast_to(g_ref[...] * (1.0 + sc_ref[...]), tile)
    shift = jnp.broadcast_to(sh_ref[...], tile)

    def apply(rows, x, inv):
        h_ref[rows, :] = (x * inv * gmod + shift).astype(BF16)

    _rowwise_rms_apply(x_ref, apply)


def _gated_postnorm_residual(f_ref, x_ref, gpost_ref, gt_ref, o_ref):
    gg = jnp.broadcast_to(gpost_ref[...] * gt_ref[...], (NORM_ROWS, f_ref.shape[1]))

    def apply(rows, f, inv):
        o_ref[rows, :] = x_ref[rows, :] + f * inv * gg

    _rowwise_rms_apply(f_ref, apply)


def _mod_kernel(c_ref, w_ref, b_ref, o_ref):
    c = c_ref[...]
    s = _silu(c).astype(BF16)
    o_ref[...] = jnp.dot(s, w_ref[...].astype(BF16), preferred_element_type=F32) + b_ref[...]


def _modulation(c_all, w_ada, b_ada):
    nb = c_all.shape[0]
    n = w_ada.shape[1]
    tn = 1024
    return pl.pallas_call(
        _mod_kernel,
        grid=(n // tn,),
        in_specs=[pl.BlockSpec((nb, D_MODEL), lambda j: (0, 0)),
                  pl.BlockSpec((D_MODEL, tn), lambda j: (0, j)),
                  pl.BlockSpec((1, tn), lambda j: (0, j))],
        out_specs=pl.BlockSpec((nb, tn), lambda j: (0, j)),
        out_shape=jax.ShapeDtypeStruct((nb, n), F32),
        compiler_params=_cparams(("arbitrary",)),
        name="adaln_mod",
    )(c_all, w_ada, b_ada.reshape(1, n))


def _rope_table_kernel(cos_ref, sin_lo_ref, sin_hi_ref):
    tm = cos_ref.shape[0]
    half = ROPE_DIM // 2
    pos = (pl.program_id(0) * tm + lax.broadcasted_iota(jnp.int32, (tm, LANES), 0)).astype(F32)
    d = lax.broadcasted_iota(jnp.int32, (tm, LANES), 1) % HEAD_DIM
    fidx = (d % half).astype(F32)
    inv = jnp.exp(fidx * (-2.0 / ROPE_DIM * math.log(ROPE_THETA)))
    ang = pos * inv
    cos = jnp.cos(ang)
    sin = jnp.sin(ang)
    cos_ref[...] = jnp.where(d < ROPE_DIM, cos, 1.0)
    sin_lo_ref[...] = jnp.where(d < half, -sin, 0.0)
    sin_hi_ref[...] = jnp.where((d >= half) & (d < ROPE_DIM), sin, 0.0)


def _rope_tables(seq):
    tm = min(seq, 2048)
    shp = jax.ShapeDtypeStruct((seq, LANES), F32)
    spec = pl.BlockSpec((tm, LANES), lambda i: (i, 0))
    return pl.pallas_call(
        _rope_table_kernel,
        grid=(seq // tm,),
        in_specs=[],
        out_specs=[spec, spec, spec],
        out_shape=[shp, shp, shp],
        compiler_params=_cparams(("arbitrary",)),
        name="rope_tables",
    )()


def _rope_slab(t, cos, sin_lo, sin_hi):
    half = ROPE_DIM // 2
    return t * cos + pltpu.roll(t, LANES - half, 1) * sin_lo + pltpu.roll(t, half, 1) * sin_hi


def _inproj_kernel(x_ref, sc_ref, sh_ref, g_ref, w_ref, wdt_ref, cos_ref, slo_ref, shi_ref,
                   proj_ref, dt_ref, h_scr, *, tn):
    j = pl.program_id(1)

    @pl.when(j == 0)
    def _():
        _modulated_prenorm(x_ref, g_ref, sc_ref, sh_ref, h_scr)
        dt_ref[...] = jnp.dot(h_scr[...], wdt_ref[...], preferred_element_type=F32)

    q_lo, q_hi = COL_Q // tn, (COL_Q + D_MODEL) // tn
    k_tile = COL_K // tn
    k_slabs = KV_WIDTH // LANES
    is_q = (j >= q_lo) & (j < q_hi)
    is_k = j == k_tile
    tm = h_scr.shape[0]
    n_parts = 2
    rows = tm // n_parts

    def project(n_rope_slabs, scale):
        for p in range(n_parts):
            rs = slice(p * rows, (p + 1) * rows)
            acc = jnp.dot(h_scr[rs, :], w_ref[...], preferred_element_type=F32)
            if n_rope_slabs == 0:
                proj_ref[rs, :] = acc.astype(BF16)
                continue
            cos, slo, shi = cos_ref[rs, :], slo_ref[rs, :], shi_ref[rs, :]
            for s in range(tn // LANES):
                t = acc[:, s * LANES:(s + 1) * LANES]
                if s < n_rope_slabs:
                    t = _rope_slab(t, cos, slo, shi)
                    if scale != 1.0:
                        t = t * scale
                proj_ref[rs, s * LANES:(s + 1) * LANES] = t.astype(BF16)

    @pl.when(is_q)
    def _():
        project(tn // LANES, HEAD_DIM ** -0.5 * LOG2E)

    @pl.when(is_k)
    def _():
        project(k_slabs, 1.0)

    @pl.when(jnp.logical_not(is_q | is_k))
    def _():
        project(0, 1.0)


def _in_projection(x2, mod4, g_pre, w_main, w_dt, tables, batch, seq):
    t_rows = batch * seq
    tm = min(seq, 1024)
    tn = 1024
    assert COL_K % tn == 0 and COL_Q % tn == 0 and D_MODEL % tn == 0 and PROJ_W % tn == 0
    per_b = seq // tm
    cos, slo, shi = tables
    tab_spec = pl.BlockSpec((tm, LANES), lambda i, j: (i % per_b, 0))
    mod_spec = lambda k: pl.BlockSpec((None, None, 1, D_MODEL), lambda i, j: (i // per_b, k, 0, 0))
    return pl.pallas_call(
        functools.partial(_inproj_kernel, tn=tn),
        grid=(t_rows // tm, PROJ_W // tn),
        in_specs=[pl.BlockSpec((tm, D_MODEL), lambda i, j: (i, 0)),
                  mod_spec(1), mod_spec(0),
                  pl.BlockSpec((1, D_MODEL), lambda i, j: (0, 0)),
                  pl.BlockSpec((D_MODEL, tn), lambda i, j: (0, j)),
                  pl.BlockSpec((D_MODEL, DT_W), lambda i, j: (0, 0)),
                  tab_spec, tab_spec, tab_spec],
        out_specs=[pl.BlockSpec((tm, tn), lambda i, j: (i, j)),
                   pl.BlockSpec((tm, DT_W), lambda i, j: (i, 0))],
        out_shape=[jax.ShapeDtypeStruct((t_rows, PROJ_W), BF16),
                   jax.ShapeDtypeStruct((t_rows, DT_W), F32)],
        scratch_shapes=[pltpu.VMEM((tm, D_MODEL), BF16)],
        compiler_params=_cparams(("arbitrary", "arbitrary")),
        name="norm_in_proj",
    )(x2, mod4, mod4, g_pre.reshape(1, D_MODEL), w_main, w_dt, cos, slo, shi)


def _attn_stages(sink_ref, q_ref, kp_ref, kc_ref, kn_ref, vp_ref, vc_ref, vn_ref, o_ref):
    i = pl.program_id(1)
    nblk = pl.num_programs(1)
    rows2 = 2 * BLOCK
    r = lax.broadcasted_iota(jnp.int32, (rows2, BLOCK), 0) % BLOCK
    c = lax.broadcasted_iota(jnp.int32, (rows2, BLOCK), 1)
    top = lax.broadcasted_iota(jnp.int32, (rows2, 1), 0) < BLOCK
    neg = jnp.float32(-jnp.inf)
    bias_p = jnp.where((c >= r) & (i > 0), 0.0, neg)
    bias_n = jnp.where((c <= r) & (i < nblk - 1), 0.0, neg)
    lane3 = lax.broadcasted_iota(jnp.int32, (3 * BLOCK, LANES), 1)
    low3 = lane3 < HEAD_DIM
    low2 = c < HEAD_DIM

    def padded_variants(p_ref, c_ref, n_ref, slab):
        sl = slice(slab * LANES, (slab + 1) * LANES)
        x = jnp.concatenate([p_ref[:, sl], c_ref[:, sl], n_ref[:, sl]], axis=0)
        xr = pltpu.bitcast(pltpu.roll(pltpu.bitcast(x, jnp.uint32), HEAD_DIM, 1), BF16)
        zero = jnp.zeros_like(x)
        return ((jnp.where(low3, x, zero), jnp.where(low3, zero, xr)),
                (jnp.where(low3, xr, zero), jnp.where(low3, zero, x)))

    variants = {}

    def scores(g):
        slab, par = divmod(g, 2)
        if slab not in variants:
            variants[slab] = (padded_variants(kp_ref, kc_ref, kn_ref, slab),
                              padded_variants(vp_ref, vc_ref, vn_ref, slab))
        k_var, v_var = variants[slab]
        qq = jnp.concatenate([q_ref[:, (2 * g) * LANES:(2 * g + 1) * LANES],
                              q_ref[:, (2 * g + 1) * LANES:(2 * g + 2) * LANES]], axis=0)
        kk = jnp.concatenate(k_var[par], axis=0)
        s = lax.dot_general(qq, kk, (((1,), (1,)), ((), ())), preferred_element_type=F32)
        return s, jnp.concatenate(v_var[par], axis=0)

    def finish(g, s, vv):
        probs, inv = [], []
        for e in range(2):
            base = e * 3 * BLOCK
            sp = s[:, base:base + BLOCK] + bias_p
            sc = s[:, base + BLOCK:base + 2 * BLOCK]
            sn = s[:, base + 2 * BLOCK:base + 3 * BLOCK] + bias_n
            sink = jnp.where(top, sink_ref[4 * g + e], sink_ref[4 * g + 2 + e]) * LOG2E
            m = jnp.maximum(jnp.max(jnp.maximum(jnp.maximum(sp, sc), sn), axis=-1, keepdims=True), sink)
            pp, pc, pn = jnp.exp2(sp - m), jnp.exp2(sc - m), jnp.exp2(sn - m)
            denom = jnp.sum(pp + pc + pn, axis=-1, keepdims=True) + jnp.exp2(sink - m)
            probs += [pp.astype(BF16), pc.astype(BF16), pn.astype(BF16)]
            inv.append(1.0 / denom)
        o = jnp.dot(jnp.concatenate(probs, axis=1), vv, preferred_element_type=F32)
        o = o * jnp.where(low2, inv[0], inv[1])
        o_ref[:, (2 * g) * LANES:(2 * g + 1) * LANES] = o[:BLOCK].astype(BF16)
        o_ref[:, (2 * g + 1) * LANES:(2 * g + 2) * LANES] = o[BLOCK:].astype(BF16)

    return scores, finish


CONV_ROWS = 512
N_SLABS = CONV_DIM // LANES


def _conv_kernel(xp_ref, xm_ref, xn_ref, cw_ref, o_ref, in_scr, out_scr):
    i = pl.program_id(1)
    n = pl.num_programs(1)
    has_prev = (i > 0).astype(F32)
    has_next = (i < n - 1).astype(F32)
    rows = CONV_ROWS
    for cs in range(N_SLABS):
        sl = slice(cs * LANES, (cs + 1) * LANES)
        in_scr[cs, 0:HALO, :] = xp_ref[:, sl].astype(F32) * has_prev
        in_scr[cs, HALO:HALO + rows, :] = xm_ref[:, sl].astype(F32)
        in_scr[cs, HALO + rows:, :] = xn_ref[:, sl].astype(F32) * has_next

    def slab_body(cs, carry):
        cw = cw_ref[cs]
        taps = [jnp.broadcast_to(cw[w:w + 1, :], (8, LANES)) for w in range(CONV_W)]
        bias = jnp.broadcast_to(cw[CONV_W:CONV_W + 1, :], (8, LANES))
        for g in range(rows // 16):
            for e in range(2):
                acc = bias
                for w in range(CONV_W):
                    start = HALO + 16 * g + e + w - CONV_W // 2
                    acc = acc + in_scr[cs, pl.ds(start, 8, stride=2), :] * taps[w]
                half = 0.5 * acc
                out_scr[cs, pl.ds(16 * g + e, 8, stride=2), :] = half + half * jnp.tanh(half)
        return carry

    lax.fori_loop(0, N_SLABS, slab_body, 0)
    for cs in range(N_SLABS):
        o_ref[:, cs * LANES:(cs + 1) * LANES] = out_scr[cs].astype(BF16)


def _conv_silu(proj3, conv_taps):
    batch, seq, _ = proj3.shape
    rows = CONV_ROWS
    hpr = rows // HALO
    nh = seq // HALO
    col = COL_XBC // CONV_DIM
    return pl.pallas_call(
        _conv_kernel,
        grid=(batch, seq // rows),
        in_specs=[pl.BlockSpec((None, HALO, CONV_DIM), lambda b, i: (b, jnp.maximum(i * hpr - 1, 0), col)),
                  pl.BlockSpec((None, rows, CONV_DIM), lambda b, i: (b, i, col)),
                  pl.BlockSpec((None, HALO, CONV_DIM), lambda b, i: (b, jnp.minimum((i + 1) * hpr, nh - 1), col)),
                  pl.BlockSpec((N_SLABS, 8, LANES), lambda b, i: (0, 0, 0))],
        out_specs=pl.BlockSpec((None, rows, CONV_DIM), lambda b, i: (b, i, 0)),
        out_shape=jax.ShapeDtypeStruct((batch, seq, CONV_DIM), BF16),
        scratch_shapes=[pltpu.VMEM((N_SLABS, rows + 2 * HALO, LANES), F32),
                        pltpu.VMEM((N_SLABS, rows, LANES), F32)],
        compiler_params=_cparams(("arbitrary", "arbitrary")),
        name="conv_silu",
    )(proj3, proj3, proj3, conv_taps)


def _ssd_direction(xa_ref, dt_ref, bias_ref, a_ref, e_ref, dskip_ref, st_ref, out_ref, *, reverse):
    lane0 = SSD_HEADS if reverse else 0
    xs = xa_ref[:, :D_INNER]
    bm = xa_ref[:, D_INNER:D_INNER + N_GROUPS * D_STATE]
    cm = xa_ref[:, D_INNER + N_GROUPS * D_STATE:]

    dt = jax.nn.softplus(dt_ref[...] + bias_ref[...])
    da = dt * a_ref[...]
    r = lax.broadcasted_iota(jnp.int32, (CHUNK, CHUNK), 0)
    c = lax.broadcasted_iota(jnp.int32, (CHUNK, CHUNK), 1)
    causal = (c >= r) if reverse else (c <= r)
    tri = causal.astype(BF16)
    d1 = da.astype(BF16)
    r1 = da - d1.astype(F32)
    d2 = r1.astype(BF16)
    d3 = (r1 - d2.astype(F32)).astype(BF16)
    cum = (jnp.dot(tri, d1, preferred_element_type=F32) + jnp.dot(tri, d2, preferred_element_type=F32)
           + jnp.dot(tri, d3, preferred_element_type=F32)) * LOG2E
    src_t = (cum - jnp.log2(dt)).T
    edge = CHUNK - 1 if not reverse else 0
    total = cum[edge:edge + 1, :]
    exp_cum = jnp.exp2(cum)
    decay = jnp.exp2(total - cum)

    e = e_ref[...]

    def expand(v):
        return jnp.dot(v.astype(BF16), e, preferred_element_type=F32)

    x_dec = xs * expand(dt * decay).astype(BF16)
    exp_cum_e = expand(exp_cum)
    tot8 = jnp.broadcast_to(jnp.exp2(total), (8, DT_W))
    tot_hi = tot8.astype(BF16)
    tot_lo = (tot8 - tot_hi.astype(F32)).astype(BF16)
    chunk_decay = (jnp.dot(tot_hi, e, preferred_element_type=F32)
                   + jnp.dot(tot_lo, e, preferred_element_type=F32))[0:1, :]

    heads_per_group = SSD_HEADS // N_GROUPS
    gw = heads_per_group * SSD_HEAD_DIM
    lane = lax.broadcasted_iota(jnp.int32, (CHUNK, LANES), 1)
    neg = jnp.float32(-jnp.inf)

    def group_matmuls(g):
        b_g = bm[:, g * D_STATE:(g + 1) * D_STATE]
        c_g = cm[:, g * D_STATE:(g + 1) * D_STATE]
        cb = lax.dot_general(c_g, b_g, (((1,), (1,)), ((), ())), preferred_element_type=F32)
        gs = slice(g * gw, (g + 1) * gw)
        st_prev = st_ref[:, gs]
        y_off = jnp.dot(c_g, st_prev.astype(BF16), preferred_element_type=F32)
        upd = lax.dot_general(b_g, x_dec[:, gs], (((0,), (0,)), ((), ())), preferred_element_type=F32)
        st_ref[:, gs] = st_prev * chunk_decay[:, gs] + upd
        return cb, y_off

    def group_finish(g, mm):
        cb, y_off = mm
        gs = slice(g * gw, (g + 1) * gw)
        y = y_off * exp_cum_e[:, gs]
        y_diag = []
        for pair in range(heads_per_group // 2):
            ms = []
            for hh in range(2):
                k = lane0 + g * heads_per_group + pair * 2 + hh
                seg = cum[:, k:k + 1] - src_t[k:k + 1, :]
                ms.append((cb * jnp.exp2(jnp.where(causal, seg, neg))).astype(BF16))
            col = g * gw + pair * LANES
            xp = xa_ref[:, col:col + LANES]
            zero = jnp.zeros_like(xp)
            rhs = jnp.concatenate([jnp.where(lane < SSD_HEAD_DIM, xp, zero),
                                   jnp.where(lane >= SSD_HEAD_DIM, xp, zero)], axis=0)
            y_diag.append(jnp.dot(jnp.concatenate(ms, axis=1), rhs, preferred_element_type=F32))
        y = y + jnp.concatenate(y_diag, axis=1)
        if dskip_ref is not None:
            y = y + dskip_ref[:, gs] * xs[:, gs].astype(F32)
        out_ref[:, gs] = y.astype(out_ref.dtype)

    return group_matmuls, group_finish


def _mixers_kernel(sink_ref, q_ref, kp_ref, kc_ref, kn_ref, vp_ref, vc_ref, vn_ref,
                   xa_f, dt_f, xa_b, dt_b, bias_ref, a_ref, e_f, e_b, dskip_ref,
                   attn_ref, yf_ref, yb_ref, st_f, st_b):
    @pl.when(pl.program_id(1) == 0)
    def _():
        st_f[...] = jnp.zeros_like(st_f)
        st_b[...] = jnp.zeros_like(st_b)

    scores, attn_finish = _attn_stages(sink_ref, q_ref, kp_ref, kc_ref, kn_ref, vp_ref, vc_ref, vn_ref, attn_ref)
    dirs = [_ssd_direction(xa_f, dt_f, bias_ref, a_ref, e_f, dskip_ref, st_f, yf_ref, reverse=False),
            _ssd_direction(xa_b, dt_b, bias_ref, a_ref, e_b, None, st_b, yb_ref, reverse=True)]
    assert N_GROUPS == N_KV_HEADS
    pend_a = scores(0)
    pend_s = [mm(0) for mm, _ in dirs]
    for g in range(N_GROUPS):
        last = g + 1 == N_GROUPS
        next_a = None if last else scores(g + 1)
        next_s = None if last else [mm(g + 1) for mm, _ in dirs]
        attn_finish(g, *pend_a)
        for (_, fin), mm_out in zip(dirs, pend_s):
            fin(g, mm_out)
        pend_a, pend_s = next_a, next_s


def _token_mixers(proj3, xact3, dt3, sinks, dt_bias, a_neg, e_f, e_b, dskip_e):
    batch, seq, _ = proj3.shape
    n = seq // BLOCK
    assert BLOCK == CHUNK
    kcol, vcol = COL_K // KV_WIDTH, COL_V // KV_WIDTH
    prev = lambda col: pl.BlockSpec((None, BLOCK, KV_WIDTH), lambda b, i: (b, jnp.maximum(i - 1, 0), col))
    cur = lambda col: pl.BlockSpec((None, BLOCK, KV_WIDTH), lambda b, i: (b, i, col))
    nxt = lambda col: pl.BlockSpec((None, BLOCK, KV_WIDTH), lambda b, i: (b, jnp.minimum(i + 1, n - 1), col))

    def ssd_specs(chunk_of):
        return [pl.BlockSpec((None, CHUNK, CONV_DIM), lambda b, i: (b, chunk_of(i), 0)),
                pl.BlockSpec((None, CHUNK, DT_W), lambda b, i: (b, chunk_of(i), 0))]

    fwd = lambda i: i
    bwd = lambda i: n - 1 - i
    const = lambda shape: pl.BlockSpec(shape, lambda b, i: (0,) * len(shape))
    row_f = pl.BlockSpec((None, BLOCK, D_MODEL), lambda b, i: (b, i, 0))
    row_b = pl.BlockSpec((None, BLOCK, D_MODEL), lambda b, i: (b, n - 1 - i, 0))
    shape = jax.ShapeDtypeStruct((batch, seq, D_MODEL), BF16)
    return pl.pallas_call(
        _mixers_kernel,
        grid=(batch, n),
        in_specs=[pl.BlockSpec(memory_space=pltpu.SMEM),
                  pl.BlockSpec((None, BLOCK, D_MODEL), lambda b, i: (b, i, COL_Q // D_MODEL)),
                  prev(kcol), cur(kcol), nxt(kcol), prev(vcol), cur(vcol), nxt(vcol)]
                 + ssd_specs(fwd) + ssd_specs(bwd)
                 + [const((1, DT_W)), const((1, DT_W)), const((DT_W, D_INNER)), const((DT_W, D_INNER)),
                    const((1, D_INNER))],
        out_specs=[row_f, row_f, row_b],
        out_shape=[shape, shape, shape],
        scratch_shapes=[pltpu.VMEM((D_STATE, D_INNER), F32), pltpu.VMEM((D_STATE, D_INNER), F32)],
        compiler_params=_cparams(("arbitrary", "arbitrary")),
        name="token_mixers",
    )(sinks, proj3, proj3, proj3, proj3, proj3, proj3, proj3, xact3, dt3, xact3, dt3,
      dt_bias, a_neg, e_f, e_b, dskip_e)


def _outproj_kernel(x_ref, gt_ref, yf_ref, yb_ref, z_ref, attn_ref, ga_ref, gs_ref, nw_ref, w_ref, gpost_ref,
                    o_ref):
    z = z_ref[...]
    y = (yf_ref[...].astype(F32) + yb_ref[...].astype(F32)) * (z * jax.nn.sigmoid(z)).astype(F32)
    gw = D_INNER // N_GROUPS
    parts = []
    for g in range(N_GROUPS):
        yg = y[:, g * gw:(g + 1) * gw]
        parts.append(yg * lax.rsqrt(jnp.mean(yg * yg, axis=-1, keepdims=True) + EPS))
    ssd = (jnp.concatenate(parts, axis=1) * nw_ref[...]).astype(BF16)
    merged = jax.nn.sigmoid(ga_ref[...]) * attn_ref[...] + jax.nn.sigmoid(gs_ref[...]) * ssd
    mix = jnp.dot(merged, w_ref[...], preferred_element_type=F32)
    nrm = mix * lax.rsqrt(jnp.mean(mix * mix, axis=-1, keepdims=True) + EPS) * gpost_ref[...]
    o_ref[...] = x_ref[...] + gt_ref[...] * nrm


def _out_projection(x2, mod4, yf2, yb2, proj2, attn2, ssd_norm_w, w_out, g_post, batch, seq):
    t_rows = batch * seq
    tm = min(seq, 512)
    per_b = seq // tm
    row = lambda col: pl.BlockSpec((tm, D_MODEL), lambda i: (i, col))
    vec = pl.BlockSpec((1, D_MODEL), lambda i: (0, 0))
    return pl.pallas_call(
        _outproj_kernel,
        grid=(t_rows // tm,),
        in_specs=[row(0),
                  pl.BlockSpec((None, None, 1, D_MODEL), lambda i: (i // per_b, 2, 0, 0)),
                  row(0), row(0), row(COL_Z // D_MODEL), row(0),
                  row(COL_GATE // D_MODEL), row(COL_GATE // D_MODEL + 1),
                  vec,
                  pl.BlockSpec((D_MODEL, D_MODEL), lambda i: (0, 0), pipeline_mode=pl.Buffered(1)),
                  vec],
        out_specs=pl.BlockSpec((tm, D_MODEL), lambda i: (i, 0)),
        out_shape=jax.ShapeDtypeStruct((t_rows, D_MODEL), F32),
        compiler_params=_cparams(("arbitrary",)),
        name="merge_out_proj",
    )(x2, mod4, yf2, yb2, proj2, attn2, proj2, proj2, ssd_norm_w.reshape(1, D_INNER), w_out,
      g_post.reshape(1, D_MODEL))


FFN_PART_ROWS = 512


def _ffn_kernel(x_ref, sc_ref, sh_ref, gt_ref, gpre_ref, wg_ref, wu_ref, wd_ref, gpost_ref, o_ref, h_scr):
    j = pl.program_id(1)

    @pl.when(j == 0)
    def _():
        _modulated_prenorm(x_ref, gpre_ref, sc_ref, sh_ref, h_scr)
        o_ref[...] = jnp.zeros_like(o_ref)

    for p in range(h_scr.shape[0] // FFN_PART_ROWS):
        rows = slice(p * FFN_PART_ROWS, (p + 1) * FFN_PART_ROWS)
        h = h_scr[rows, :]
        gate = jnp.dot(h, wg_ref[...], preferred_element_type=F32)
        up = jnp.dot(h, wu_ref[...], preferred_element_type=F32)
        act = (_silu(gate) * up).astype(BF16)
        o_ref[rows, :] += jnp.dot(act, wd_ref[...], preferred_element_type=F32)

    @pl.when(j == pl.num_programs(1) - 1)
    def _():
        _gated_postnorm_residual(o_ref, x_ref, gpost_ref, gt_ref, o_ref)


def _ffn(x2, mod4, g_pre, w_gu, w_d, g_post, batch, seq):
    t_rows = batch * seq
    tm = min(seq, 1024)
    tf = 512
    assert tm % FFN_PART_ROWS == 0
    up0 = D_FF // tf
    per_b = seq // tm
    mod_spec = lambda k: pl.BlockSpec((None, None, 1, D_MODEL), lambda i, j: (i // per_b, k, 0, 0))
    vec = pl.BlockSpec((1, D_MODEL), lambda i, j: (0, 0))
    return pl.pallas_call(
        _ffn_kernel,
        grid=(t_rows // tm, D_FF // tf),
        in_specs=[pl.BlockSpec((tm, D_MODEL), lambda i, j: (i, 0)),
                  mod_spec(4), mod_spec(3), mod_spec(5), vec,
                  pl.BlockSpec((D_MODEL, tf), lambda i, j: (0, j)),
                  pl.BlockSpec((D_MODEL, tf), lambda i, j: (0, up0 + j)),
                  pl.BlockSpec((tf, D_MODEL), lambda i, j: (j, 0)),
                  vec],
        out_specs=pl.BlockSpec((tm, D_MODEL), lambda i, j: (i, 0)),
        out_shape=jax.ShapeDtypeStruct((t_rows, D_MODEL), F32),
        scratch_shapes=[pltpu.VMEM((tm, D_MODEL), BF16)],
        compiler_params=_cparams(("arbitrary", "arbitrary")),
        name="swiglu_ffn",
    )(x2, mod4, mod4, mod4, g_pre.reshape(1, D_MODEL), w_gu, w_gu, w_d, g_post.reshape(1, D_MODEL))


def _prep_weights(w_in, conv_w, conv_b, a_log_f, a_log_b, dt_bias_f, dt_bias_b, d_skip, w_out, w_gu, w_down):
    o_q, o_k, o_v = 0, D_MODEL, D_MODEL + KV_WIDTH
    o_z = o_v + KV_WIDTH
    o_xbc = o_z + D_INNER
    o_dt = o_xbc + CONV_DIM
    o_gate = o_dt + 2 * SSD_HEADS
    seg = lambda off, width: w_in[:, off:off + width]
    w_main = jnp.concatenate([seg(o_xbc, CONV_DIM), seg(o_gate, 2 * D_MODEL), seg(o_q, D_MODEL),
                              seg(o_z, D_INNER), seg(o_k, KV_WIDTH), seg(o_v, KV_WIDTH)], axis=1).astype(BF16)
    w_dt = jnp.pad(seg(o_dt, 2 * SSD_HEADS), ((0, 0), (0, DT_W - 2 * SSD_HEADS))).astype(BF16)
    pad_l = lambda v: jnp.pad(v, (0, DT_W - v.shape[0])).reshape(1, DT_W)
    dt_bias = pad_l(jnp.concatenate([dt_bias_f, dt_bias_b]))
    a_neg = pad_l(-jnp.exp(jnp.concatenate([a_log_f, a_log_b])))
    taps8 = jnp.concatenate([conv_w, conv_b[None, :], jnp.zeros((8 - CONV_W - 1, CONV_DIM), F32)], axis=0)
    conv_taps = taps8.reshape(8, N_SLABS, LANES).transpose(1, 0, 2)
    head_of_lane = jnp.arange(D_INNER) // SSD_HEAD_DIM
    rows = jnp.arange(DT_W)[:, None]
    e_f = (rows == head_of_lane[None, :]).astype(BF16)
    e_b = (rows == head_of_lane[None, :] + SSD_HEADS).astype(BF16)
    dskip_e = jnp.repeat(d_skip, SSD_HEAD_DIM).reshape(1, D_INNER)
    return dict(w_main=w_main, w_dt=w_dt, dt_bias=dt_bias, a_neg=a_neg, conv_taps=conv_taps,
                e_f=e_f, e_b=e_b, dskip_e=dskip_e,
                w_out=w_out.astype(BF16), w_gu=w_gu.astype(BF16),
                w_d=w_down.astype(BF16))


def _encoder_layer(x, mod4, tables, pw, g_pre1, g_post1, ssd_norm_w, sinks, g_pre2, g_post2):
    batch, seq, _ = x.shape
    x2 = x.reshape(batch * seq, D_MODEL)
    proj2, dt2 = _in_projection(x2, mod4, g_pre1, pw["w_main"], pw["w_dt"], tables, batch, seq)
    proj3 = proj2.reshape(batch, seq, PROJ_W)
    dt3 = dt2.reshape(batch, seq, DT_W)
    xact3 = _conv_silu(proj3, pw["conv_taps"])
    attn, yf, yb = _token_mixers(proj3, xact3, dt3, sinks, pw["dt_bias"], pw["a_neg"], pw["e_f"], pw["e_b"],
                                 pw["dskip_e"])
    x1 = _out_projection(x2, mod4, yf.reshape(batch * seq, D_INNER), yb.reshape(batch * seq, D_INNER), proj2,
                         attn.reshape(batch * seq, D_MODEL), ssd_norm_w, pw["w_out"], g_post1, batch, seq)
    out = _ffn(x1, mod4, g_pre2, pw["w_gu"], pw["w_d"], g_post2, batch, seq)
    return out.reshape(batch, seq, D_MODEL)


def kernel(x_prompt, x_sample, c_prompt, c_sample, w_ada, b_ada, g_pre1, g_post1, w_in, conv_w, conv_b, a_log_f,
           a_log_b, dt_bias_f, dt_bias_b, d_skip, ssd_norm_w, sinks, w_out, g_pre2, g_post2, w_gu, w_down):
    assert w_ada.shape[0] == 1, "single-layer problem"
    nb_p, nb_s = c_prompt.shape[0], c_sample.shape[0]
    nb = -(-(nb_p + nb_s) // 8) * 8
    c_all = jnp.pad(jnp.concatenate([c_prompt, c_sample], axis=0), ((0, nb - nb_p - nb_s), (0, 0)))
    mod = _modulation(c_all, w_ada[0], b_ada[0]).reshape(nb, 6, 1, D_MODEL)
    pw = _prep_weights(w_in[0], conv_w[0], conv_b[0], a_log_f[0], a_log_b[0], dt_bias_f[0], dt_bias_b[0],
                       d_skip[0], w_out[0], w_gu[0], w_down[0])
    outs = []
    for x, mod4 in ((x_prompt, mod[:nb_p]), (x_sample, mod[nb_p:nb_p + nb_s])):
        tables = _rope_tables(x.shape[1])
        outs.append(_encoder_layer(x, mod4, tables, pw, g_pre1[0], g_post1[0], ssd_norm_w[0], sinks[0],
                                   g_pre2[0], g_post2[0]))
    return tuple(outs)
```
